```python
import jax
import jax.numpy as jnp
from jax import lax
import numpy as np


D_MODEL = 1024
BATCH = 32
SEQ = 2048
DEPTH = 4

GRID_W = 64
CTX_LEN = 256
HEAD_DIM = 128
N_HEADS = D_MODEL // HEAD_DIM
N_KV_HEADS = N_HEADS // 4
Q_GROUP = N_HEADS // N_KV_HEADS
Q_BLOCK = 128
ATTN_SCALE = HEAD_DIM ** -0.5
ROPE_THETA = 10000.0
ROPE_PAIRS = HEAD_DIM // 4
CONV_WIDTH = 31
CONV_CH = D_MODEL
POOL_WINDOWS = (2, 4, 8, 16)
POOL_GROUPS = len(POOL_WINDOWS)
POOL_CH = D_MODEL
POOL_GROUP_CH = POOL_CH // POOL_GROUPS
N_BRANCH = 3
D_FF = 2816
FFN_CONV_WIDTH = 3
N_MOD = 6
ATTN_W = N_HEADS * HEAD_DIM
KV_W = N_KV_HEADS * HEAD_DIM
Q_END = ATTN_W
K_END = Q_END + KV_W
V_END = K_END + KV_W
CONV_END = V_END + 2 * CONV_CH
POOL_END = CONV_END + POOL_CH
D_IN = POOL_END + N_BRANCH * D_MODEL
DEEPNORM_ALPHA = (2 * DEPTH) ** 0.25
DEEPNORM_BETA = (8 * DEPTH) ** -0.25
LN_EPS = 1e-5
RMS_EPS = 1e-6

kernel_name = "hybrid_gated_parallel_diffusion_trunk"


def layer_norm(t, g, b):
    tf = t.astype(jnp.float32)
    mu = jnp.mean(tf, axis=-1, keepdims=True)
    var = jnp.mean(jnp.square(tf - mu), axis=-1, keepdims=True)
    return ((tf - mu) * lax.rsqrt(var + LN_EPS)).astype(t.dtype) * g + b


def rms_norm(t, g):
    tf = t.astype(jnp.float32)
    ms = jnp.mean(jnp.square(tf), axis=-1, keepdims=True)
    return (tf * lax.rsqrt(ms + RMS_EPS)).astype(t.dtype) * g


def ada_mod(cond, w, b, n):
    m = jax.nn.silu(cond) @ w[:, :n * D_MODEL] + b[:n * D_MODEL]
    return m.reshape(m.shape[:-1] + (n, D_MODEL))


def modulate(h, shift, scale):
    return h * (1.0 + scale) + shift


def axial_rope_tables(n_tokens):
    n_rows = n_tokens // GRID_W
    row = jnp.repeat(jnp.arange(n_rows, dtype=jnp.float32), GRID_W)
    col = jnp.tile(jnp.arange(GRID_W, dtype=jnp.float32), n_rows)
    inv_freq = ROPE_THETA ** (-jnp.arange(ROPE_PAIRS, dtype=jnp.float32) / ROPE_PAIRS)
    ang = jnp.stack([row, col], axis=-1)[..., None] * inv_freq
    return jnp.cos(ang), jnp.sin(ang)


def apply_rope(t, cos, sin):
    b, n, h, _ = t.shape
    tf = t.astype(jnp.float32).reshape(b, n, h, 2, 2, ROPE_PAIRS)
    t1, t2 = tf[..., 0, :], tf[..., 1, :]
    c = cos[None, :, None]
    s = sin[None, :, None]
    out = jnp.stack([t1 * c - t2 * s, t1 * s + t2 * c], axis=-2)
    return out.reshape(t.shape).astype(t.dtype)


def kv_heads(z_kv, k_gain):
    lead = z_kv.shape[:-1]
    k = rms_norm(z_kv[..., :KV_W].reshape(lead + (N_KV_HEADS, HEAD_DIM)), k_gain)
    v = z_kv[..., KV_W:].reshape(lead + (N_KV_HEADS, HEAD_DIM))
    return k, v


def qkv_heads(z, q_gain, k_gain):
    q = rms_norm(z[..., :Q_END].reshape(z.shape[:-1] + (N_HEADS, HEAD_DIM)), q_gain)
    k, v = kv_heads(z[..., Q_END:V_END], k_gain)
    return q, k, v


def latent_attention(q, k_lat, v_lat, k_ctx, v_ctx):
    b, n = q.shape[0], q.shape[1]
    k_all = jnp.concatenate([k_ctx, k_lat], axis=1)
    v_all = jnp.concatenate([v_ctx, v_lat], axis=1)
    n_blk = n // Q_BLOCK
    qb = jnp.moveaxis(q.reshape(b, n_blk, Q_BLOCK, N_KV_HEADS, Q_GROUP, HEAD_DIM), 1, 0)

    def block(qi):
        s = jnp.einsum('bqhgd,bkhd->bhgqk', qi, k_all, preferred_element_type=jnp.float32) * ATTN_SCALE
        p = jax.nn.softmax(s, axis=-1).astype(v_all.dtype)
        return jnp.einsum('bhgqk,bkhd->bqhgd', p, v_all)

    o = lax.map(block, qb)
    return jnp.moveaxis(o, 0, 1).reshape(b, n, ATTN_W)


def context_attention(q, k, v):
    b, n = q.shape[0], q.shape[1]
    qg = q.reshape(b, n, N_KV_HEADS, Q_GROUP, HEAD_DIM)
    s = jnp.einsum('bqhgd,bkhd->bhgqk', qg, k, preferred_element_type=jnp.float32) * ATTN_SCALE
    p = jax.nn.softmax(s, axis=-1).astype(v.dtype)
    return jnp.einsum('bhgqk,bkhd->bqhgd', p, v).reshape(b, n, ATTN_W)


def depthwise_conv(t, w, b):
    k = w.shape[0]
    y = lax.conv_general_dilated(
        t, w[:, None, :], window_strides=(1,), padding=[((k - 1) // 2, k // 2)],
        dimension_numbers=('NWC', 'WIO', 'NWC'), feature_group_count=t.shape[-1])
    return y + b


def conformer_conv(u, dw_w, dw_b, ln_g, ln_b, pw_w, pw_b):
    a, gt = jnp.split(u, 2, axis=-1)
    h = depthwise_conv(a * jax.nn.sigmoid(gt), dw_w, dw_b)
    h = jax.nn.silu(layer_norm(h, ln_g, ln_b))
    return h @ pw_w + pw_b


def multiscale_pool(u, pool_w, pool_scale):
    b, n, _ = u.shape
    uf = u.astype(jnp.float32)
    cs = jnp.pad(jnp.cumsum(uf, axis=1), ((0, 0), (1, 0), (0, 0)))
    t = jnp.arange(n)
    outs = []
    for g, w in enumerate(POOL_WINDOWS):
        lo = jnp.clip(t - w // 2, 0, n)
        hi = jnp.clip(t - w // 2 + w, 0, n)
        sl = slice(g * POOL_GROUP_CH, (g + 1) * POOL_GROUP_CH)
        seg = cs[:, :, sl]
        win_sum = jnp.take(seg, hi, axis=1) - jnp.take(seg, lo, axis=1)
        cnt = (hi - lo).astype(jnp.float32)[None, :, None]
        outs.append(win_sum / cnt - uf[:, :, sl])
    pooled = jnp.stack(outs, axis=2).astype(u.dtype)
    mixed = jnp.einsum('blgi,gio->blgo', pooled, pool_w)
    return mixed.reshape(b, n, POOL_CH) * pool_scale


def merge_branches(z, attn_o, conv_dw_w, conv_dw_b, conv_ln_g, conv_ln_b, conv_pw_w, conv_pw_b,
                   pool_w, pool_scale, w_out, b_out):
    conv_o = conformer_conv(z[..., V_END:CONV_END], conv_dw_w, conv_dw_b, conv_ln_g, conv_ln_b,
                            conv_pw_w, conv_pw_b)
    pool_o = multiscale_pool(z[..., CONV_END:POOL_END], pool_w, pool_scale)
    gates = jax.nn.sigmoid(z[..., POOL_END:].reshape(z.shape[:-1] + (N_BRANCH, D_MODEL)))
    m = gates[..., 0, :] * attn_o + gates[..., 1, :] * conv_o + gates[..., 2, :] * pool_o
    return m @ w_out + b_out


def conv_ffn(h, w_up, dw_w, dw_b, w_down):
    a, u = jnp.split(h @ w_up, 2, axis=-1)
    a = depthwise_conv(a, dw_w, dw_b)
    return (jax.nn.silu(a) * u) @ w_down


def setup_inputs(seed: int = 0) -> dict:
    key = jax.random.key(seed)
    ks = iter(jax.random.split(key, 32))

    def nrm(shape, scale):
        return jax.random.normal(next(ks), shape, jnp.float32) * scale

    L = DEPTH
    D = D_MODEL
    return {
        'x': nrm((BATCH, SEQ, D), 1.0),
        'c': nrm((BATCH, D), 1.0),
        'ctx': nrm((BATCH, CTX_LEN, D), 1.0),
        'c_ctx': nrm((D,), 1.0),
        'w_ada': nrm((L, D, N_MOD * D), 0.5 * D ** -0.5),
        'b_ada': nrm((L, N_MOD * D), 0.02),
        'w_in': nrm((L, D, D_IN), D ** -0.5),
        'b_in': nrm((L, D_IN), 0.02),
        'q_gain': 1.0 + nrm((L, HEAD_DIM), 0.02),
        'k_gain': 1.0 + nrm((L, HEAD_DIM), 0.02),
        'conv_dw_w': nrm((L, CONV_WIDTH, CONV_CH), CONV_WIDTH ** -0.5),
        'conv_dw_b': nrm((L, CONV_CH), 0.02),
        'conv_ln_g': 1.0 + nrm((L, CONV_CH), 0.02),
        'conv_ln_b': nrm((L, CONV_CH), 0.02),
        'conv_pw_w': nrm((L, CONV_CH, D), CONV_CH ** -0.5),
        'conv_pw_b': nrm((L, D), 0.02),
        'pool_w': nrm((L, POOL_GROUPS, POOL_GROUP_CH, POOL_GROUP_CH), POOL_GROUP_CH ** -0.5),
        'pool_scale': 1.0 + nrm((L, POOL_CH), 0.1),
        'w_out': nrm((L, D, D), DEEPNORM_BETA * D ** -0.5),
        'b_out': nrm((L, D), 0.02),
        'ln1_g': 1.0 + nrm((L, D), 0.02),
        'ln1_b': nrm((L, D), 0.02),
        'ln2_g': 1.0 + nrm((L, D), 0.02),
        'ln2_b': nrm((L, D), 0.02),
        'w_up': nrm((L, D, 2 * D_FF), D ** -0.5),
        'ffn_dw_w': nrm((L, FFN_CONV_WIDTH, D_FF), FFN_CONV_WIDTH ** -0.5),
        'ffn_dw_b': nrm((L, D_FF), 0.02),
        'w_down': nrm((L, D_FF, D), DEEPNORM_BETA * D_FF ** -0.5),
    }


def reference(x, c, ctx, c_ctx, w_ada, b_ada, w_in, b_in, q_gain, k_gain,
              conv_dw_w, conv_dw_b, conv_ln_g, conv_ln_b, conv_pw_w, conv_pw_b,
              pool_w, pool_scale, w_out, b_out, ln1_g, ln1_b, ln2_g, ln2_b,
              w_up, ffn_dw_w, ffn_dw_b, w_down):
    cos, sin = axial_rope_tables(x.shape[1])
    for l in range(DEPTH):
        last = l == DEPTH - 1

        def mix(z, attn_o):
            return merge_branches(z, attn_o, conv_dw_w[l], conv_dw_b[l], conv_ln_g[l], conv_ln_b[l],
                                  conv_pw_w[l], conv_pw_b[l], pool_w[l], pool_scale[l],
                                  w_out[l], b_out[l])

        def ffn(h):
            return conv_ffn(h, w_up[l], ffn_dw_w[l], ffn_dw_b[l], w_down[l])

        ml = ada_mod(c, w_ada[l], b_ada[l], N_MOD)[:, None]
        mc = ada_mod(c_ctx, w_ada[l], b_ada[l], 2 if last else N_MOD)

        hl = modulate(x, ml[..., 0, :], ml[..., 1, :])
        hc = modulate(ctx, mc[0], mc[1])
        zl = hl @ w_in[l] + b_in[l]
        q_l, k_l, v_l = qkv_heads(zl, q_gain[l], k_gain[l])
        q_l = apply_rope(q_l, cos, sin)
        k_l = apply_rope(k_l, cos, sin)
        if last:
            k_c, v_c = kv_heads(hc @ w_in[l][:, Q_END:V_END] + b_in[l][Q_END:V_END], k_gain[l])
        else:
            zc = hc @ w_in[l] + b_in[l]
            q_c, k_c, v_c = qkv_heads(zc, q_gain[l], k_gain[l])
        attn_l = latent_attention(q_l, k_l, v_l, k_c, v_c)
        x = layer_norm(DEEPNORM_ALPHA * x + ml[..., 2, :] * mix(zl, attn_l), ln1_g[l], ln1_b[l])

        x = layer_norm(DEEPNORM_ALPHA * x + ml[..., 5, :] * ffn(modulate(x, ml[..., 3, :], ml[..., 4, :])),
                       ln2_g[l], ln2_b[l])

        if not last:
            attn_c = context_attention(q_c, k_c, v_c)
            ctx = layer_norm(DEEPNORM_ALPHA * ctx + mc[2] * mix(zc, attn_c), ln1_g[l], ln1_b[l])
            ctx = layer_norm(DEEPNORM_ALPHA * ctx + mc[5] * ffn(modulate(ctx, mc[3], mc[4])),
                             ln2_g[l], ln2_b[l])
    return x
```

```python
import functools

import jax
import jax.numpy as jnp
from jax import lax
from jax.experimental import pallas as pl
from jax.experimental.pallas import tpu as pltpu

_F32 = jnp.float32
_BF16 = jnp.bfloat16

D_MODEL = 1024
HEAD_DIM = 128
N_HEADS = 8
N_KV_HEADS = 2
Q_GROUP = N_HEADS // N_KV_HEADS
GRID_W = 64
ROPE_THETA = 10000.0
ROPE_PAIRS = HEAD_DIM // 4
ATTN_SCALE = HEAD_DIM ** -0.5
CONV_WIDTH = 31
POOL_WINDOWS = (2, 4, 8, 16)
POOL_GROUP_CH = D_MODEL // len(POOL_WINDOWS)
D_FF = 2816
N_MOD = 6
ATTN_W = N_HEADS * HEAD_DIM
KV_W = N_KV_HEADS * HEAD_DIM
Q_END = ATTN_W
K_END = Q_END + KV_W
V_END = K_END + KV_W
CONV_END = V_END + 2 * D_MODEL
POOL_END = CONV_END + D_MODEL
D_IN = POOL_END + 3 * D_MODEL
DEEPNORM_ALPHA = (2 * 4) ** 0.25
LN_EPS = 1e-5
RMS_EPS = 1e-6

SUBLANES = 8
BF16_ROWS = 16
VMEM_LIMIT_BYTES = 58 * 1024 * 1024

HALO = BF16_ROWS
FFN_HALO = SUBLANES
FFN_CHUNK = 256
CONV_ROWS = 64
CONV_LANES = 256


def _sigmoid(v):
    return 1.0 / (1.0 + jnp.exp(-v))


def _layer_norm(t, g, b):
    mu = jnp.mean(t, axis=-1, keepdims=True)
    tc = t - mu
    var = jnp.mean(tc * tc, axis=-1, keepdims=True)
    return tc * lax.rsqrt(var + LN_EPS) * g + b


def _resident(block_shape, index):
    return pl.BlockSpec(block_shape, lambda *_: index, pipeline_mode=pl.Buffered(1))


def _params(n_grid):
    return pltpu.CompilerParams(dimension_semantics=("arbitrary",) * n_grid,
                                vmem_limit_bytes=VMEM_LIMIT_BYTES)


def _ada_kernel(cond_ref, w_ref, b_ref, o_ref):
    cnd = cond_ref[...]
    act = (cnd * _sigmoid(cnd)).astype(_BF16)
    o_ref[...] = jnp.dot(act, w_ref[...].astype(_BF16), preferred_element_type=_F32) + b_ref[...]


def _ada_all_layers(cond, w_ada, b_ada):
    nl, d, n = w_ada.shape
    r = cond.shape[0]
    nb = 1536
    return pl.pallas_call(
        _ada_kernel,
        grid=(nl, n // nb),
        in_specs=[pl.BlockSpec((r, d), lambda l, j: (0, 0)),
                  pl.BlockSpec((None, d, nb), lambda l, j: (l, 0, j)),
                  pl.BlockSpec((None, 1, nb), lambda l, j: (l, 0, j))],
        out_specs=pl.BlockSpec((None, r, nb), lambda l, j: (l, 0, j)),
        out_shape=jax.ShapeDtypeStruct((nl, r, n), _F32),
        compiler_params=_params(2),
        name="ada_mod",
    )(cond, w_ada, b_ada.reshape(nl, 1, n))


def _inproj_kernel(x_ref, mod_ref, w_ref, b_ref, qg_ref, kg_ref, cos_ref, sin_ref, *out_refs,
                   use_rope, kv_only):
    x = x_ref[...]
    shift = mod_ref[0:1, :]
    scale = mod_ref[1:2, :]
    h = (x * (1.0 + scale) + shift).astype(_BF16)

    def proj(c0, n):
        return (jnp.dot(h, w_ref[:, c0:c0 + n], preferred_element_type=_F32)
                + b_ref[:, c0:c0 + n])

    if use_rope:
        cos = cos_ref[...]
        sin = sin_ref[...]
        lane = lax.broadcasted_iota(jnp.int32, cos.shape, 1)
        low_half = (lane & ROPE_PAIRS) == 0

    def norm_rope(t, gain, out_scale):
        ms = jnp.mean(t * t, axis=-1, keepdims=True)
        tn = t * lax.rsqrt(ms + RMS_EPS) * gain
        if use_rope:
            partner = jnp.where(low_half,
                                pltpu.roll(tn, HEAD_DIM - ROPE_PAIRS, 1),
                                pltpu.roll(tn, ROPE_PAIRS, 1))
            tn = tn * cos + partner * sin
        if out_scale != 1.0:
            tn = tn * out_scale
        return tn.astype(_BF16)

    if kv_only:
        k_ref, v_ref = out_refs
    else:
        q_ref, k_ref, v_ref, a_ref, p_ref, g_ref = out_refs
        zq = proj(0, ATTN_W)
        qg = qg_ref[...]
        for hd in range(N_HEADS):
            sl = slice(hd * HEAD_DIM, (hd + 1) * HEAD_DIM)
            q_ref[:, sl] = norm_rope(zq[:, sl], qg, ATTN_SCALE)

    zkv = proj(Q_END, 2 * KV_W)
    kg = kg_ref[...]
    for hd in range(N_KV_HEADS):
        sl = slice(hd * HEAD_DIM, (hd + 1) * HEAD_DIM)
        k_ref[:, sl] = norm_rope(zkv[:, sl], kg, 1.0)
    v_ref[...] = zkv[:, KV_W:].astype(_BF16)

    if not kv_only:
        half = D_MODEL // 2
        for j in range(2):
            a = proj(V_END + j * half, half)
            gt = proj(V_END + D_MODEL + j * half, half)
            a_ref[:, j * half:(j + 1) * half] = (a * _sigmoid(gt)).astype(_BF16)
        p_ref[...] = proj(CONV_END, D_MODEL).astype(_BF16)
        for j in range(3):
            g_ref[:, j * D_MODEL:(j + 1) * D_MODEL] = _sigmoid(
                proj(POOL_END + j * D_MODEL, D_MODEL)).astype(_BF16)


def _inproj(x, mod, per_batch_mod, l, w_in, b_in, q_gain, k_gain, cos_t, sin_t, *,
            use_rope, kv_only, tile):
    bsz, n, d = x.shape
    nt = n // tile
    mod_idx = (lambda b, i: (l, b, 0, 0)) if per_batch_mod else (lambda b, i: (l, 0, 0, 0))
    tok = lambda w: pl.BlockSpec((None, tile, w), lambda b, i: (b, i, 0))
    widths = (KV_W, KV_W) if kv_only else (ATTN_W, KV_W, KV_W, D_MODEL, D_MODEL, 3 * D_MODEL)
    return pl.pallas_call(
        functools.partial(_inproj_kernel, use_rope=use_rope, kv_only=kv_only),
        grid=(bsz, nt),
        in_specs=[tok(d),
                  pl.BlockSpec((None, None, N_MOD, d), mod_idx),
                  _resident((None, d, D_IN), (l, 0, 0)),
                  _resident((None, 1, D_IN), (l, 0, 0)),
                  _resident((None, 1, HEAD_DIM), (l, 0, 0)),
                  _resident((None, 1, HEAD_DIM), (l, 0, 0)),
                  pl.BlockSpec((tile, HEAD_DIM), lambda b, i: (i, 0)),
                  pl.BlockSpec((tile, HEAD_DIM), lambda b, i: (i, 0))],
        out_specs=[tok(w) for w in widths],
        out_shape=[jax.ShapeDtypeStruct((bsz, n, w), _BF16) for w in widths],
        compiler_params=_params(2),
        name="in_proj",
    )(x, mod, w_in, b_in, q_gain, k_gain, cos_t, sin_t)


def _attn_kernel(q_ref, *refs, n_src):
    kv_refs = refs[:2 * n_src]
    o_ref = refs[2 * n_src]
    nt_dims = (((1,), (1,)), ((), ()))
    for g in range(Q_GROUP):
        sl = slice(g * HEAD_DIM, (g + 1) * HEAD_DIM)
        qg = q_ref[:, sl]
        scores = [lax.dot_general(qg, kv_refs[2 * s][...], nt_dims, preferred_element_type=_F32)
                  for s in range(n_src)]
        m = jnp.max(scores[0], axis=-1, keepdims=True)
        for s in scores[1:]:
            m = jnp.maximum(m, jnp.max(s, axis=-1, keepdims=True))
        den = None
        acc = None
        for s in range(n_src):
            p = jnp.exp(scores[s] - m)
            ps = jnp.sum(p, axis=-1, keepdims=True)
            pv = jnp.dot(p.astype(_BF16), kv_refs[2 * s + 1][...], preferred_element_type=_F32)
            den = ps if den is None else den + ps
            acc = pv if acc is None else acc + pv
        o_ref[:, sl] = (acc / den).astype(_BF16)


def _attention(q, sources, *, tile):
    bsz, n, _ = q.shape
    gw = Q_GROUP * HEAD_DIM
    in_specs = [pl.BlockSpec((None, tile, gw), lambda b, h, i: (b, i, h))]
    args = [q]
    for k, v in sources:
        nk = k.shape[1]
        in_specs += [pl.BlockSpec((None, nk, HEAD_DIM), lambda b, h, i: (b, 0, h))] * 2
        args += [k, v]
    return pl.pallas_call(
        functools.partial(_attn_kernel, n_src=len(sources)),
        grid=(bsz, N_KV_HEADS, n // tile),
        in_specs=in_specs,
        out_specs=pl.BlockSpec((None, tile, gw), lambda b, h, i: (b, i, h)),
        out_shape=jax.ShapeDtypeStruct((bsz, n, ATTN_W), _BF16),
        compiler_params=_params(3),
        name="attention",
    )(*args)


def _mix_kernel(x_ref, attn_ref, a_ref, ap_ref, an_ref, p_ref, pp_ref, pn_ref, g_ref, mod_ref,
                dww_ref, dwb_ref, clg_ref, clb_ref, pww_ref, pwb_ref, poolw_ref, pscale_ref,
                wout_ref, bout_ref, lg_ref, lb_ref, o_ref,
                abuf, sbuf, cbuf, pbuf, l1, l2, l3, l4, *, tile, seq_len):
    i = pl.program_id(1)
    nt = pl.num_programs(1)
    has_prev = i > 0
    has_next = i < nt - 1
    t = tile

    abuf[0:HALO, :] = jnp.where(has_prev, ap_ref[...].astype(_F32), 0.0)
    abuf[HALO:HALO + t, :] = a_ref[...].astype(_F32)
    abuf[HALO + t:2 * HALO + t, :] = jnp.where(has_next, an_ref[...].astype(_F32), 0.0)
    pbuf[0:HALO, :] = jnp.where(has_prev, pp_ref[...].astype(_F32), 0.0)
    pbuf[HALO:HALO + t, :] = p_ref[...].astype(_F32)
    pbuf[HALO + t:2 * HALO + t, :] = jnp.where(has_next, pn_ref[...].astype(_F32), 0.0)
    pbuf[2 * HALO + t:, :] = jnp.zeros((SUBLANES, D_MODEL), _F32)

    span = t + 3 * SUBLANES
    for lb in range(D_MODEL // CONV_LANES):
        lanes = slice(lb * CONV_LANES, (lb + 1) * CONV_LANES)
        for j in range(1, SUBLANES):
            sbuf[j] = abuf[j:j + span, lanes]

        def conv_rows(it, carry, lanes=lanes):
            base = pl.multiple_of(it * CONV_ROWS, CONV_ROWS)
            acc = jnp.zeros((CONV_ROWS, CONV_LANES), _F32)
            for k in range(CONV_WIDTH):
                o = k + 1
                hi, lo = o // SUBLANES, o % SUBLANES
                if lo == 0:
                    win = abuf[pl.ds(base + o, CONV_ROWS), lanes]
                else:
                    win = sbuf[lo, pl.ds(base + hi * SUBLANES, CONV_ROWS), :]
                acc = acc + win * dww_ref[k:k + 1, lanes]
            cbuf[pl.ds(base, CONV_ROWS), lanes] = acc + dwb_ref[:, lanes]
            return carry

        lax.fori_loop(0, t // CONV_ROWS, conv_rows, 0)

    hn = _layer_norm(cbuf[...], clg_ref[...], clb_ref[...])
    hs = (hn * _sigmoid(hn)).astype(_BF16)
    conv_o = jnp.dot(hs, pww_ref[...], preferred_element_type=_F32) + pwb_ref[...]

    gc = POOL_GROUP_CH
    n1, n2, n3, n4 = t + 4 * SUBLANES, t + 3 * SUBLANES, t + 2 * SUBLANES, t + SUBLANES
    l1[...] = pbuf[0:n1, :] + pbuf[1:n1 + 1, :]
    l2[...] = l1[0:n2, gc:] + l1[2:n2 + 2, gc:]
    l3[...] = l2[0:n3, gc:] + l2[4:n3 + 4, gc:]
    l4[...] = l3[0:n4, gc:] + l3[8:n4 + 8, gc:]
    win_sums = (l1[HALO - 1:HALO - 1 + t, 0:gc],
                l2[HALO - 2:HALO - 2 + t, 0:gc],
                l3[HALO - 4:HALO - 4 + t, 0:gc],
                l4[HALO - 8:HALO - 8 + t, :])
    pos = i * t + lax.broadcasted_iota(jnp.int32, (t, 1), 0)
    mixed = []
    for g, w in enumerate(POOL_WINDOWS):
        lo = jnp.clip(pos - w // 2, 0, seq_len)
        hi = jnp.clip(pos - w // 2 + w, 0, seq_len)
        cnt = (hi - lo).astype(_F32)
        pooled = win_sums[g] / cnt - pbuf[HALO:HALO + t, g * gc:(g + 1) * gc]
        mixed.append(jnp.dot(pooled.astype(_BF16), poolw_ref[g], preferred_element_type=_F32))
    pool_o = jnp.concatenate(mixed, axis=-1) * pscale_ref[...]

    m = (g_ref[:, 0:D_MODEL].astype(_F32) * attn_ref[...].astype(_F32)
         + g_ref[:, D_MODEL:2 * D_MODEL].astype(_F32) * conv_o
         + g_ref[:, 2 * D_MODEL:].astype(_F32) * pool_o)
    y = jnp.dot(m.astype(_BF16), wout_ref[...], preferred_element_type=_F32) + bout_ref[...]
    gate = mod_ref[2:3, :]
    o_ref[...] = _layer_norm(DEEPNORM_ALPHA * x_ref[...] + gate * y, lg_ref[...], lb_ref[...])


def _mix(x, attn, a, p, g, mod, per_batch_mod, l, wts, *, tile):
    bsz, n, d = x.shape
    nt = n // tile
    hb = tile // HALO
    last_hb = n // HALO - 1
    mod_idx = (lambda b, i: (l, b, 0, 0)) if per_batch_mod else (lambda b, i: (l, 0, 0, 0))
    tok = lambda w: pl.BlockSpec((None, tile, w), lambda b, i: (b, i, 0))
    prev = pl.BlockSpec((None, HALO, d), lambda b, i: (b, jnp.maximum(i * hb - 1, 0), 0))
    nxt = pl.BlockSpec((None, HALO, d), lambda b, i: (b, jnp.minimum((i + 1) * hb, last_hb), 0))
    vec = _resident((None, 1, d), (l, 0, 0))
    mat = _resident((None, d, d), (l, 0, 0))
    gc = POOL_GROUP_CH
    return pl.pallas_call(
        functools.partial(_mix_kernel, tile=tile, seq_len=n),
        grid=(bsz, nt),
        in_specs=[tok(d), tok(d), tok(d), prev, nxt, tok(d), prev, nxt, tok(3 * d),
                  pl.BlockSpec((None, None, N_MOD, d), mod_idx),
                  _resident((None, CONV_WIDTH, d), (l, 0, 0)), vec, vec, vec, mat, vec,
                  _resident((None, len(POOL_WINDOWS), gc, gc), (l, 0, 0, 0)), vec,
                  mat, vec, vec, vec],
        out_specs=tok(d),
        out_shape=jax.ShapeDtypeStruct((bsz, n, d), _F32),
        scratch_shapes=[
            pltpu.VMEM((tile + 2 * HALO, d), _F32),
            pltpu.VMEM((SUBLANES, tile + 3 * SUBLANES, CONV_LANES), _F32),
            pltpu.VMEM((tile, d), _F32),
            pltpu.VMEM((tile + 2 * HALO + SUBLANES, d), _F32),
            pltpu.VMEM((tile + 4 * SUBLANES, d), _F32),
            pltpu.VMEM((tile + 3 * SUBLANES, d - gc), _F32),
            pltpu.VMEM((tile + 2 * SUBLANES, d - 2 * gc), _F32),
            pltpu.VMEM((tile + SUBLANES, d - 3 * gc), _F32),
        ],
        compiler_params=_params(2),
        name="mix",
    )(x, attn, a, a, a, p, p, p, g, mod,
      wts["conv_dw_w"], wts["conv_dw_b"], wts["conv_ln_g"], wts["conv_ln_b"],
      wts["conv_pw_w"], wts["conv_pw_b"], wts["pool_w"], wts["pool_scale"],
      wts["w_out"], wts["b_out"], wts["ln1_g"], wts["ln1_b"])


def _ffn_kernel(x_ref, xp_ref, xn_ref, mod_ref, wup_ref, dww_ref, dwb_ref, wdn_ref, lg_ref, lb_ref,
                o_ref, hext, hb, abuf, acc, *, tile):
    i = pl.program_id(1)
    nt = pl.num_programs(1)
    t = tile
    shift = mod_ref[3:4, :]
    scale = mod_ref[4:5, :]
    gate = mod_ref[5:6, :]

    def modulate(v):
        return v * (1.0 + scale) + shift

    hext[0:FFN_HALO, :] = jnp.where(i > 0, modulate(xp_ref[...]), 0.0)
    hext[FFN_HALO:FFN_HALO + t, :] = modulate(x_ref[...])
    hext[FFN_HALO + t:, :] = jnp.where(i < nt - 1, modulate(xn_ref[...]), 0.0)
    hb[...] = hext[...].astype(_BF16)
    acc[...] = jnp.zeros(acc.shape, _F32)

    def chunk(c, carry):
        z = jnp.dot(hb[...], wup_ref[c], preferred_element_type=_F32)
        abuf[...] = z[:, 0:FFN_CHUNK]
        dw = dww_ref[c]
        y = (abuf[FFN_HALO - 1:FFN_HALO - 1 + t, :] * dw[0:1, :]
             + abuf[FFN_HALO:FFN_HALO + t, :] * dw[1:2, :]
             + abuf[FFN_HALO + 1:FFN_HALO + 1 + t, :] * dw[2:3, :]
             + dwb_ref[c])
        u = z[FFN_HALO:FFN_HALO + t, FFN_CHUNK:]
        gact = (y * _sigmoid(y) * u).astype(_BF16)
        acc[...] += jnp.dot(gact, wdn_ref[c], preferred_element_type=_F32)
        return carry

    lax.fori_loop(0, D_FF // FFN_CHUNK, chunk, 0)
    o_ref[...] = _layer_norm(DEEPNORM_ALPHA * x_ref[...] + gate * acc[...], lg_ref[...], lb_ref[...])


def _ffn(x, mod, per_batch_mod, l, wts, *, tile):
    bsz, n, d = x.shape
    nt = n // tile
    hb = tile // FFN_HALO
    last_hb = n // FFN_HALO - 1
    nc = D_FF // FFN_CHUNK
    mod_idx = (lambda b, i: (l, b, 0, 0)) if per_batch_mod else (lambda b, i: (l, 0, 0, 0))
    tok = pl.BlockSpec((None, tile, d), lambda b, i: (b, i, 0))
    prev = pl.BlockSpec((None, FFN_HALO, d), lambda b, i: (b, jnp.maximum(i * hb - 1, 0), 0))
    nxt = pl.BlockSpec((None, FFN_HALO, d), lambda b, i: (b, jnp.minimum((i + 1) * hb, last_hb), 0))
    vec = _resident((None, 1, d), (l, 0, 0))
    return pl.pallas_call(
        functools.partial(_ffn_kernel, tile=tile),
        grid=(bsz, nt),
        in_specs=[tok, prev, nxt,
                  pl.BlockSpec((None, None, N_MOD, d), mod_idx),
                  _resident((None, nc, d, 2 * FFN_CHUNK), (l, 0, 0, 0)),
                  _resident((None, nc, 3, FFN_CHUNK), (l, 0, 0, 0)),
                  _resident((None, nc, 1, FFN_CHUNK), (l, 0, 0, 0)),
                  _resident((None, nc, FFN_CHUNK, d), (l, 0, 0, 0)),
                  vec, vec],
        out_specs=tok,
        out_shape=jax.ShapeDtypeStruct((bsz, n, d), _F32),
        scratch_shapes=[
            pltpu.VMEM((tile + 2 * FFN_HALO, d), _F32),
            pltpu.VMEM((tile + 2 * FFN_HALO, d), _BF16),
            pltpu.VMEM((tile + 2 * FFN_HALO, FFN_CHUNK), _F32),
            pltpu.VMEM((tile, d), _F32),
        ],
        compiler_params=_params(2),
        name="conv_ffn",
    )(x, x, x, mod, wts["w_up"], wts["ffn_dw_w"], wts["ffn_dw_b"], wts["w_down"],
      wts["ln2_g"], wts["ln2_b"])


def _rope_tables(n_tokens):
    n_rows = n_tokens // GRID_W
    row = jnp.repeat(jnp.arange(n_rows, dtype=_F32), GRID_W)
    col = jnp.tile(jnp.arange(GRID_W, dtype=_F32), n_rows)
    inv_freq = ROPE_THETA ** (-jnp.arange(ROPE_PAIRS, dtype=_F32) / ROPE_PAIRS)
    ang = jnp.stack([row, col], axis=-1)[..., None] * inv_freq
    cos, sin = jnp.cos(ang), jnp.sin(ang)
    cos_t = jnp.concatenate([cos[:, 0], cos[:, 0], cos[:, 1], cos[:, 1]], axis=-1)
    sin_t = jnp.concatenate([-sin[:, 0], sin[:, 0], -sin[:, 1], sin[:, 1]], axis=-1)
    return cos_t, sin_t


def _pick_tile(n, target):
    return target if n % target == 0 else n


def kernel(x, c, ctx, c_ctx, w_ada, b_ada, w_in, b_in, q_gain, k_gain, conv_dw_w, conv_dw_b,
           conv_ln_g, conv_ln_b, conv_pw_w, conv_pw_b, pool_w, pool_scale, w_out, b_out,
           ln1_g, ln1_b, ln2_g, ln2_b, w_up, ffn_dw_w, ffn_dw_b, w_down):
    bsz, n, d = x.shape
    n_ctx = ctx.shape[1]
    nl = w_in.shape[0]
    nc = D_FF // FFN_CHUNK
    assert d == D_MODEL and n % GRID_W == 0 and n % CONV_ROWS == 0 and n_ctx % CONV_ROWS == 0

    n_cond = -(-(bsz + 1) // SUBLANES) * SUBLANES
    cond = jnp.zeros((n_cond, d), _F32).at[:bsz].set(c).at[bsz].set(c_ctx)
    mods = _ada_all_layers(cond, w_ada, b_ada)
    mod_lat = mods[:, :bsz].reshape(nl, bsz, N_MOD, d)
    mod_ctx = mods[:, bsz:bsz + 1].reshape(nl, 1, N_MOD, d)

    vec = lambda v: v.reshape(nl, 1, v.shape[-1])
    w_up_a = w_up[:, :, :D_FF].reshape(nl, d, nc, FFN_CHUNK)
    w_up_u = w_up[:, :, D_FF:].reshape(nl, d, nc, FFN_CHUNK)
    wts = {
        "conv_dw_w": conv_dw_w, "conv_dw_b": vec(conv_dw_b),
        "conv_ln_g": vec(conv_ln_g), "conv_ln_b": vec(conv_ln_b),
        "conv_pw_w": conv_pw_w.astype(_BF16), "conv_pw_b": vec(conv_pw_b),
        "pool_w": pool_w.astype(_BF16), "pool_scale": vec(pool_scale),
        "w_out": w_out.astype(_BF16), "b_out": vec(b_out),
        "ln1_g": vec(ln1_g), "ln1_b": vec(ln1_b), "ln2_g": vec(ln2_g), "ln2_b": vec(ln2_b),
        "w_up": jnp.concatenate([w_up_a, w_up_u], axis=-1).transpose(0, 2, 1, 3).astype(_BF16),
        "ffn_dw_w": ffn_dw_w.reshape(nl, 3, nc, FFN_CHUNK).transpose(0, 2, 1, 3),
        "ffn_dw_b": ffn_dw_b.reshape(nl, nc, 1, FFN_CHUNK),
        "w_down": w_down.reshape(nl, nc, FFN_CHUNK, d).astype(_BF16),
    }
    w_in_b = w_in.astype(_BF16)
    b_in_v, q_gain_v, k_gain_v = vec(b_in), vec(q_gain), vec(k_gain)
    cos_t, sin_t = _rope_tables(n)
    ones_t = jnp.ones((n_ctx, HEAD_DIM), _F32)

    t_lat = _pick_tile(n, 512)
    t_ctx = _pick_tile(n_ctx, 512)

    for l in range(nl):
        last = l == nl - 1
        inproj = functools.partial(_inproj, l=l, w_in=w_in_b, b_in=b_in_v, q_gain=q_gain_v,
                                   k_gain=k_gain_v)
        q_l, k_l, v_l, a_l, p_l, g_l = inproj(x, mod_lat, True, cos_t=cos_t, sin_t=sin_t,
                                              use_rope=True, kv_only=False, tile=t_lat)
        if last:
            k_c, v_c = inproj(ctx, mod_ctx, False, cos_t=ones_t, sin_t=ones_t,
                              use_rope=False, kv_only=True, tile=t_ctx)
        else:
            q_c, k_c, v_c, a_c, p_c, g_c = inproj(ctx, mod_ctx, False, cos_t=ones_t, sin_t=ones_t,
                                                  use_rope=False, kv_only=False, tile=t_ctx)
        attn_l = _attention(q_l, [(k_c, v_c), (k_l, v_l)], tile=t_lat)
        x = _mix(x, attn_l, a_l, p_l, g_l, mod_lat, True, l, wts, tile=t_lat)
        x = _ffn(x, mod_lat, True, l, wts, tile=t_lat)
        if not last:
            attn_c = _attention(q_c, [(k_c, v_c)], tile=t_ctx)
            ctx = _mix(ctx, attn_c, a_c, p_c, g_c, mod_ctx, False, l, wts, tile=t_ctx)
            ctx = _ffn(ctx, mod_ctx, False, l, wts, tile=t_ctx)
    return x
```

```python
import functools

import jax
import jax.numpy as jnp
from jax import lax
from jax.experimental import pallas as pl
from jax.experimental.pallas import tpu as pltpu

_F32 = jnp.float32
_BF16 = jnp.bfloat16

D_MODEL = 1024
HEAD_DIM = 128
N_HEADS = 8
N_KV_HEADS = 2
Q_GROUP = N_HEADS // N_KV_HEADS
GRID_W = 64
ROPE_THETA = 10000.0
ROPE_PAIRS = HEAD_DIM // 4
ATTN_SCALE = HEAD_DIM ** -0.5
LOG2_E = 1.4426950408889634
Q_SCALE = ATTN_SCALE * LOG2_E
CONV_WIDTH = 31
POOL_WINDOWS = (2, 4, 8, 16)
POOL_GROUP_CH = D_MODEL // len(POOL_WINDOWS)
D_FF = 2816
N_MOD = 6
ATTN_W = N_HEADS * HEAD_DIM
KV_W = N_KV_HEADS * HEAD_DIM
Q_END = ATTN_W
K_END = Q_END + KV_W
V_END = K_END + KV_W
CONV_END = V_END + 2 * D_MODEL
POOL_END = CONV_END + D_MODEL
D_IN = POOL_END + 3 * D_MODEL
DEEPNORM_ALPHA = (2 * 4) ** 0.25
LN_EPS = 1e-5
RMS_EPS = 1e-6

SUBLANES = 8
BF16_ROWS = 16
VMEM_LIMIT_BYTES = 58 * 1024 * 1024

HALO = BF16_ROWS
FFN_HALO = SUBLANES
FFN_CHUNK = 256
CONV_ROWS = 64
CONV_LANES = 256


def _sigmoid(v):
    return 1.0 / (1.0 + jnp.exp(-v))


def _layer_norm(t, g, b):
    mu = jnp.mean(t, axis=-1, keepdims=True)
    tc = t - mu
    var = jnp.mean(tc * tc, axis=-1, keepdims=True)
    return tc * lax.rsqrt(var + LN_EPS) * g + b


def _resident(block_shape, index):
    return pl.BlockSpec(block_shape, lambda *_: index, pipeline_mode=pl.Buffered(1))


def _params(n_grid):
    return pltpu.CompilerParams(dimension_semantics=("arbitrary",) * n_grid,
                                vmem_limit_bytes=VMEM_LIMIT_BYTES)


def _ada_kernel(cond_ref, w_ref, b_ref, o_ref):
    cnd = cond_ref[...]
    act = (cnd * _sigmoid(cnd)).astype(_BF16)
    o_ref[...] = jnp.dot(act, w_ref[...].astype(_BF16), preferred_element_type=_F32) + b_ref[...]


def _ada_all_layers(cond, w_ada, b_ada):
    nl, d, n = w_ada.shape
    r = cond.shape[0]
    nb = 1536
    return pl.pallas_call(
        _ada_kernel,
        grid=(nl, n // nb),
        in_specs=[pl.BlockSpec((r, d), lambda l, j: (0, 0)),
                  pl.BlockSpec((None, d, nb), lambda l, j: (l, 0, j)),
                  pl.BlockSpec((None, 1, nb), lambda l, j: (l, 0, j))],
        out_specs=pl.BlockSpec((None, r, nb), lambda l, j: (l, 0, j)),
        out_shape=jax.ShapeDtypeStruct((nl, r, n), _F32),
        compiler_params=_params(2),
        name="ada_mod",
    )(cond, w_ada, b_ada.reshape(nl, 1, n))


def _inproj_kernel(x_ref, mod_ref, w_ref, b_ref, qg_ref, kg_ref, cos_ref, sin_ref, *out_refs,
                   use_rope, kv_only):
    x = x_ref[...]
    shift = mod_ref[0:1, :]
    scale = mod_ref[1:2, :]
    h = (x * (1.0 + scale) + shift).astype(_BF16)

    def proj(c0, n):
        return (jnp.dot(h, w_ref[:, c0:c0 + n], preferred_element_type=_F32)
                + b_ref[:, c0:c0 + n])

    if use_rope:
        cos = cos_ref[...]
        sin = sin_ref[...]
        lane = lax.broadcasted_iota(jnp.int32, cos.shape, 1)
        low_half = (lane & ROPE_PAIRS) == 0

    def norm_rope(t, gain, out_scale):
        ms = jnp.mean(t * t, axis=-1, keepdims=True)
        tn = t * lax.rsqrt(ms + RMS_EPS) * gain
        if use_rope:
            partner = jnp.where(low_half,
                                pltpu.roll(tn, HEAD_DIM - ROPE_PAIRS, 1),
                                pltpu.roll(tn, ROPE_PAIRS, 1))
            tn = tn * cos + partner * sin
        if out_scale != 1.0:
            tn = tn * out_scale
        return tn.astype(_BF16)

    if kv_only:
        k_ref, v_ref = out_refs
    else:
        q_ref, k_ref, v_ref, a_ref, p_ref, g_ref = out_refs
        zq = proj(0, ATTN_W)
        qg = qg_ref[...]
        for hd in range(N_HEADS):
            sl = slice(hd * HEAD_DIM, (hd + 1) * HEAD_DIM)
            q_ref[:, sl] = norm_rope(zq[:, sl], qg, Q_SCALE)

    zkv = proj(Q_END, 2 * KV_W)
    kg = kg_ref[...]
    for hd in range(N_KV_HEADS):
        sl = slice(hd * HEAD_DIM, (hd + 1) * HEAD_DIM)
        k_ref[:, sl] = norm_rope(zkv[:, sl], kg, 1.0)
    v_ref[...] = zkv[:, KV_W:].astype(_BF16)

    if not kv_only:
        half = D_MODEL // 2
        for j in range(2):
            a = proj(V_END + j * half, half)
            gt = proj(V_END + D_MODEL + j * half, half)
            a_ref[:, j * half:(j + 1) * half] = (a * _sigmoid(gt)).astype(_BF16)
        p_ref[...] = proj(CONV_END, D_MODEL).astype(_BF16)
        for j in range(3):
            g_ref[:, j * D_MODEL:(j + 1) * D_MODEL] = _sigmoid(
                proj(POOL_END + j * D_MODEL, D_MODEL)).astype(_BF16)


def _inproj(x, mod, per_batch_mod, l, w_in, b_in, q_gain, k_gain, cos_t, sin_t, *,
            use_rope, kv_only, tile):
    bsz, n, d = x.shape
    nt = n // tile
    mod_idx = (lambda b, i: (l, b, 0, 0)) if per_batch_mod else (lambda b, i: (l, 0, 0, 0))
    tok = lambda w: pl.BlockSpec((None, tile, w), lambda b, i: (b, i, 0))
    widths = (KV_W, KV_W) if kv_only else (ATTN_W, KV_W, KV_W, D_MODEL, D_MODEL, 3 * D_MODEL)
    return pl.pallas_call(
        functools.partial(_inproj_kernel, use_rope=use_rope, kv_only=kv_only),
        grid=(bsz, nt),
        in_specs=[tok(d),
                  pl.BlockSpec((None, None, N_MOD, d), mod_idx),
                  _resident((None, d, D_IN), (l, 0, 0)),
                  _resident((None, 1, D_IN), (l, 0, 0)),
                  _resident((None, 1, HEAD_DIM), (l, 0, 0)),
                  _resident((None, 1, HEAD_DIM), (l, 0, 0)),
                  pl.BlockSpec((tile, HEAD_DIM), lambda b, i: (i, 0)),
                  pl.BlockSpec((tile, HEAD_DIM), lambda b, i: (i, 0))],
        out_specs=[tok(w) for w in widths],
        out_shape=[jax.ShapeDtypeStruct((bsz, n, w), _BF16) for w in widths],
        compiler_params=_params(2),
        name="in_proj",
    )(x, mod, w_in, b_in, q_gain, k_gain, cos_t, sin_t)


def _attn_kernel(q_ref, *refs, n_src):
    kv_refs = refs[:2 * n_src]
    o_ref = refs[2 * n_src]
    vext = refs[2 * n_src + 1:]
    nt_dims = (((1,), (1,)), ((), ()))

    @pl.when(pl.program_id(2) == 0)
    def _():
        for s in range(n_src):
            v = kv_refs[2 * s + 1][...]
            vext[s][:, 0:HEAD_DIM] = v
            vext[s][:, HEAD_DIM:] = jnp.ones(v.shape, v.dtype)

    def qk(g):
        qg = q_ref[:, g * HEAD_DIM:(g + 1) * HEAD_DIM]
        return [lax.dot_general(qg, kv_refs[2 * s][...], nt_dims, preferred_element_type=_F32)
                for s in range(n_src)]

    scores_next = qk(0)
    for g in range(Q_GROUP):
        sl = slice(g * HEAD_DIM, (g + 1) * HEAD_DIM)
        scores = scores_next
        scores_next = qk(g + 1) if g + 1 < Q_GROUP else None
        m = jnp.max(scores[0], axis=-1, keepdims=True)
        for s in scores[1:]:
            m = jnp.maximum(m, jnp.max(s, axis=-1, keepdims=True))
        acc = None
        for s in range(n_src):
            p = jnp.exp2(scores[s] - m).astype(_BF16)
            pv = jnp.dot(p, vext[s][...], preferred_element_type=_F32)
            acc = pv if acc is None else acc + pv
        o_ref[:, sl] = (acc[:, 0:HEAD_DIM] / acc[:, HEAD_DIM:]).astype(_BF16)


def _attention(q, sources, *, tile):
    bsz, n, _ = q.shape
    gw = Q_GROUP * HEAD_DIM
    in_specs = [pl.BlockSpec((None, tile, gw), lambda b, h, i: (b, i, h))]
    args = [q]
    for k, v in sources:
        nk = k.shape[1]
        in_specs += [pl.BlockSpec((None, nk, HEAD_DIM), lambda b, h, i: (b, 0, h))] * 2
        args += [k, v]
    return pl.pallas_call(
        functools.partial(_attn_kernel, n_src=len(sources)),
        grid=(bsz, N_KV_HEADS, n // tile),
        in_specs=in_specs,
        out_specs=pl.BlockSpec((None, tile, gw), lambda b, h, i: (b, i, h)),
        out_shape=jax.ShapeDtypeStruct((bsz, n, ATTN_W), _BF16),
        scratch_shapes=[pltpu.VMEM((k.shape[1], 2 * HEAD_DIM), _BF16) for k, _ in sources],
        compiler_params=_params(3),
        name="attention",
    )(*args)


def _mix_kernel(x_ref, attn_ref, a_ref, ap_ref, an_ref, p_ref, pp_ref, pn_ref, g_ref, mod_ref,
                dww_ref, dwb_ref, clg_ref, clb_ref, pww_ref, pwb_ref, poolw_ref, pscale_ref,
                wout_ref, bout_ref, lg_ref, lb_ref, o_ref,
                abuf, sbuf, cbuf, pbuf, l1, l2, l3, l4, *, tile, seq_len):
    i = pl.program_id(1)
    nt = pl.num_programs(1)
    has_prev = i > 0
    has_next = i < nt - 1
    t = tile

    abuf[0:HALO, :] = jnp.where(has_prev, ap_ref[...].astype(_F32), 0.0)
    abuf[HALO:HALO + t, :] = a_ref[...].astype(_F32)
    abuf[HALO + t:2 * HALO + t, :] = jnp.where(has_next, an_ref[...].astype(_F32), 0.0)
    pbuf[0:HALO, :] = jnp.where(has_prev, pp_ref[...].astype(_F32), 0.0)
    pbuf[HALO:HALO + t, :] = p_ref[...].astype(_F32)
    pbuf[HALO + t:2 * HALO + t, :] = jnp.where(has_next, pn_ref[...].astype(_F32), 0.0)
    pbuf[2 * HALO + t:, :] = jnp.zeros((SUBLANES, D_MODEL), _F32)

    span = t + 3 * SUBLANES
    for lb in range(D_MODEL // CONV_LANES):
        lanes = slice(lb * CONV_LANES, (lb + 1) * CONV_LANES)
        for j in range(1, SUBLANES):
            sbuf[j] = abuf[j:j + span, lanes]

        def conv_rows(it, carry, lanes=lanes):
            base = pl.multiple_of(it * CONV_ROWS, CONV_ROWS)
            acc = jnp.zeros((CONV_ROWS, CONV_LANES), _F32)
            for k in range(CONV_WIDTH):
                o = k + 1
                hi, lo = o // SUBLANES, o % SUBLANES
                if lo == 0:
                    win = abuf[pl.ds(base + o, CONV_ROWS), lanes]
                else:
                    win = sbuf[lo, pl.ds(base + hi * SUBLANES, CONV_ROWS), :]
                acc = acc + win * dww_ref[k:k + 1, lanes]
            cbuf[pl.ds(base, CONV_ROWS), lanes] = acc + dwb_ref[:, lanes]
            return carry

        lax.fori_loop(0, t // CONV_ROWS, conv_rows, 0)

    hn = _layer_norm(cbuf[...], clg_ref[...], clb_ref[...])
    hs = (hn * _sigmoid(hn)).astype(_BF16)
    conv_o = jnp.dot(hs, pww_ref[...], preferred_element_type=_F32) + pwb_ref[...]

    gc = POOL_GROUP_CH
    n1, n2, n3, n4 = t + 4 * SUBLANES, t + 3 * SUBLANES, t + 2 * SUBLANES, t + SUBLANES
    l1[...] = pbuf[0:n1, :] + pbuf[1:n1 + 1, :]
    l2[...] = l1[0:n2, gc:] + l1[2:n2 + 2, gc:]
    l3[...] = l2[0:n3, gc:] + l2[4:n3 + 4, gc:]
    l4[...] = l3[0:n4, gc:] + l3[8:n4 + 8, gc:]
    win_sums = (l1[HALO - 1:HALO - 1 + t, 0:gc],
                l2[HALO - 2:HALO - 2 + t, 0:gc],
                l3[HALO - 4:HALO - 4 + t, 0:gc],
                l4[HALO - 8:HALO - 8 + t, :])
    pos = i * t + lax.broadcasted_iota(jnp.int32, (t, 1), 0)
    mixed = []
    for g, w in enumerate(POOL_WINDOWS):
        lo = jnp.clip(pos - w // 2, 0, seq_len)
        hi = jnp.clip(pos - w // 2 + w, 0, seq_len)
        cnt = (hi - lo).astype(_F32)
        pooled = win_sums[g] / cnt - pbuf[HALO:HALO + t, g * gc:(g + 1) * gc]
        mixed.append(jnp.dot(pooled.astype(_BF16), poolw_ref[g], preferred_element_type=_F32))
    pool_o = jnp.concatenate(mixed, axis=-1) * pscale_ref[...]

    m = (g_ref[:, 0:D_MODEL].astype(_F32) * attn_ref[...].astype(_F32)
         + g_ref[:, D_MODEL:2 * D_MODEL].astype(_F32) * conv_o
         + g_ref[:, 2 * D_MODEL:].astype(_F32) * pool_o)
    y = jnp.dot(m.astype(_BF16), wout_ref[...], preferred_element_type=_F32) + bout_ref[...]
    gate = mod_ref[2:3, :]
    o_ref[...] = _layer_norm(DEEPNORM_ALPHA * x_ref[...] + gate * y, lg_ref[...], lb_ref[...])


def _mix(x, attn, a, p, g, mod, per_batch_mod, l, wts, *, tile):
    bsz, n, d = x.shape
    nt = n // tile
    hb = tile // HALO
    last_hb = n // HALO - 1
    mod_idx = (lambda b, i: (l, b, 0, 0)) if per_batch_mod else (lambda b, i: (l, 0, 0, 0))
    tok = lambda w: pl.BlockSpec((None, tile, w), lambda b, i: (b, i, 0))
    prev = pl.BlockSpec((None, HALO, d), lambda b, i: (b, jnp.maximum(i * hb - 1, 0), 0))
    nxt = pl.BlockSpec((None, HALO, d), lambda b, i: (b, jnp.minimum((i + 1) * hb, last_hb), 0))
    vec = _resident((None, 1, d), (l, 0, 0))
    mat = _resident((None, d, d), (l, 0, 0))
    gc = POOL_GROUP_CH
    return pl.pallas_call(
        functools.partial(_mix_kernel, tile=tile, seq_len=n),
        grid=(bsz, nt),
        in_specs=[tok(d), tok(d), tok(d), prev, nxt, tok(d), prev, nxt, tok(3 * d),
                  pl.BlockSpec((None, None, N_MOD, d), mod_idx),
                  _resident((None, CONV_WIDTH, d), (l, 0, 0)), vec, vec, vec, mat, vec,
                  _resident((None, len(POOL_WINDOWS), gc, gc), (l, 0, 0, 0)), vec,
                  mat, vec, vec, vec],
        out_specs=tok(d),
        out_shape=jax.ShapeDtypeStruct((bsz, n, d), _F32),
        scratch_shapes=[
            pltpu.VMEM((tile + 2 * HALO, d), _F32),
            pltpu.VMEM((SUBLANES, tile + 3 * SUBLANES, CONV_LANES), _F32),
            pltpu.VMEM((tile, d), _F32),
            pltpu.VMEM((tile + 2 * HALO + SUBLANES, d), _F32),
            pltpu.VMEM((tile + 4 * SUBLANES, d), _F32),
            pltpu.VMEM((tile + 3 * SUBLANES, d - gc), _F32),
            pltpu.VMEM((tile + 2 * SUBLANES, d - 2 * gc), _F32),
            pltpu.VMEM((tile + SUBLANES, d - 3 * gc), _F32),
        ],
        compiler_params=_params(2),
        name="mix",
    )(x, attn, a, a, a, p, p, p, g, mod,
      wts["conv_dw_w"], wts["conv_dw_b"], wts["conv_ln_g"], wts["conv_ln_b"],
      wts["conv_pw_w"], wts["conv_pw_b"], wts["pool_w"], wts["pool_scale"],
      wts["w_out"], wts["b_out"], wts["ln1_g"], wts["ln1_b"])


def _ffn_kernel(x_ref, xp_ref, xn_ref, mod_ref, wup_ref, dww_ref, dwb_ref, wdn_ref, lg_ref, lb_ref,
                o_ref, hext, hb, abuf, acc, *, tile):
    i = pl.program_id(1)
    nt = pl.num_programs(1)
    t = tile
    shift = mod_ref[3:4, :]
    scale = mod_ref[4:5, :]
    gate = mod_ref[5:6, :]

    def modulate(v):
        return v * (1.0 + scale) + shift

    hext[0:FFN_HALO, :] = jnp.where(i > 0, modulate(xp_ref[...]), 0.0)
    hext[FFN_HALO:FFN_HALO + t, :] = modulate(x_ref[...])
    hext[FFN_HALO + t:, :] = jnp.where(i < nt - 1, modulate(xn_ref[...]), 0.0)
    hb[...] = hext[...].astype(_BF16)
    acc[...] = jnp.zeros(acc.shape, _F32)

    def up(c):
        return jnp.dot(hb[...], wup_ref[c], preferred_element_type=_F32)

    nc = D_FF // FFN_CHUNK
    z = up(0)
    for c in range(nc):
        z_next = up(c + 1) if c + 1 < nc else None
        ab = abuf.at[c % 2]
        ab[...] = z[:, 0:FFN_CHUNK]
        dw = dww_ref[c]
        y = (ab[FFN_HALO - 1:FFN_HALO - 1 + t, :] * dw[0:1, :]
             + ab[FFN_HALO:FFN_HALO + t, :] * dw[1:2, :]
             + ab[FFN_HALO + 1:FFN_HALO + 1 + t, :] * dw[2:3, :]
             + dwb_ref[c])
        u = z[FFN_HALO:FFN_HALO + t, FFN_CHUNK:]
        gact = (y * _sigmoid(y) * u).astype(_BF16)
        acc[...] += jnp.dot(gact, wdn_ref[c], preferred_element_type=_F32)
        z = z_next
    o_ref[...] = _layer_norm(DEEPNORM_ALPHA * x_ref[...] + gate * acc[...], lg_ref[...], lb_ref[...])


def _ffn(x, mod, per_batch_mod, l, wts, *, tile):
    bsz, n, d = x.shape
    nt = n // tile
    hb = tile // FFN_HALO
    last_hb = n // FFN_HALO - 1
    nc = D_FF // FFN_CHUNK
    mod_idx = (lambda b, i: (l, b, 0, 0)) if per_batch_mod else (lambda b, i: (l, 0, 0, 0))
    tok = pl.BlockSpec((None, tile, d), lambda b, i: (b, i, 0))
    prev = pl.BlockSpec((None, FFN_HALO, d), lambda b, i: (b, jnp.maximum(i * hb - 1, 0), 0))
    nxt = pl.BlockSpec((None, FFN_HALO, d), lambda b, i: (b, jnp.minimum((i + 1) * hb, last_hb), 0))
    vec = _resident((None, 1, d), (l, 0, 0))
    return pl.pallas_call(
        functools.partial(_ffn_kernel, tile=tile),
        grid=(bsz, nt),
        in_specs=[tok, prev, nxt,
                  pl.BlockSpec((None, None, N_MOD, d), mod_idx),
                  _resident((None, nc, d, 2 * FFN_CHUNK), (l, 0, 0, 0)),
                  _resident((None, nc, 3, FFN_CHUNK), (l, 0, 0, 0)),
                  _resident((None, nc, 1, FFN_CHUNK), (l, 0, 0, 0)),
                  _resident((None, nc, FFN_CHUNK, d), (l, 0, 0, 0)),
                  vec, vec],
        out_specs=tok,
        out_shape=jax.ShapeDtypeStruct((bsz, n, d), _F32),
        scratch_shapes=[
            pltpu.VMEM((tile + 2 * FFN_HALO, d), _F32),
            pltpu.VMEM((tile + 2 * FFN_HALO, d), _BF16),
            pltpu.VMEM((2, tile + 2 * FFN_HALO, FFN_CHUNK), _F32),
            pltpu.VMEM((tile, d), _F32),
        ],
        compiler_params=_params(2),
        name="conv_ffn",
    )(x, x, x, mod, wts["w_up"], wts["ffn_dw_w"], wts["ffn_dw_b"], wts["w_down"],
      wts["ln2_g"], wts["ln2_b"])


def _rope_tables(n_tokens):
    n_rows = n_tokens // GRID_W
    row = jnp.repeat(jnp.arange(n_rows, dtype=_F32), GRID_W)
    col = jnp.tile(jnp.arange(GRID_W, dtype=_F32), n_rows)
    inv_freq = ROPE_THETA ** (-jnp.arange(ROPE_PAIRS, dtype=_F32) / ROPE_PAIRS)
    ang = jnp.stack([row, col], axis=-1)[..., None] * inv_freq
    cos, sin = jnp.cos(ang), jnp.sin(ang)
    cos_t = jnp.concatenate([cos[:, 0], cos[:, 0], cos[:, 1], cos[:, 1]], axis=-1)
    sin_t = jnp.concatenate([-sin[:, 0], sin[:, 0], -sin[:, 1], sin[:, 1]], axis=-1)
    return cos_t, sin_t


def _pick_tile(n, target):
    return target if n % target == 0 else n


def kernel(x, c, ctx, c_ctx, w_ada, b_ada, w_in, b_in, q_gain, k_gain, conv_dw_w, conv_dw_b,
           conv_ln_g, conv_ln_b, conv_pw_w, conv_pw_b, pool_w, pool_scale, w_out, b_out,
           ln1_g, ln1_b, ln2_g, ln2_b, w_up, ffn_dw_w, ffn_dw_b, w_down):
    bsz, n, d = x.shape
    n_ctx = ctx.shape[1]
    nl = w_in.shape[0]
    nc = D_FF // FFN_CHUNK
    assert d == D_MODEL and n % GRID_W == 0 and n % CONV_ROWS == 0 and n_ctx % CONV_ROWS == 0

    n_cond = -(-(bsz + 1) // SUBLANES) * SUBLANES
    cond = jnp.zeros((n_cond, d), _F32).at[:bsz].set(c).at[bsz].set(c_ctx)
    mods = _ada_all_layers(cond, w_ada, b_ada)
    mod_lat = mods[:, :bsz].reshape(nl, bsz, N_MOD, d)
    mod_ctx = mods[:, bsz:bsz + 1].reshape(nl, 1, N_MOD, d)

    vec = lambda v: v.reshape(nl, 1, v.shape[-1])
    w_up_a = w_up[:, :, :D_FF].reshape(nl, d, nc, FFN_CHUNK)
    w_up_u = w_up[:, :, D_FF:].reshape(nl, d, nc, FFN_CHUNK)
    wts = {
        "conv_dw_w": conv_dw_w, "conv_dw_b": vec(conv_dw_b),
        "conv_ln_g": vec(conv_ln_g), "conv_ln_b": vec(conv_ln_b),
        "conv_pw_w": conv_pw_w.astype(_BF16), "conv_pw_b": vec(conv_pw_b),
        "pool_w": pool_w.astype(_BF16), "pool_scale": vec(pool_scale),
        "w_out": w_out.astype(_BF16), "b_out": vec(b_out),
        "ln1_g": vec(ln1_g), "ln1_b": vec(ln1_b), "ln2_g": vec(ln2_g), "ln2_b": vec(ln2_b),
        "w_up": jnp.concatenate([w_up_a, w_up_u], axis=-1).transpose(0, 2, 1, 3).astype(_BF16),
        "ffn_dw_w": ffn_dw_w.reshape(nl, 3, nc, FFN_CHUNK).transpose(0, 2, 1, 3),
        "ffn_dw_b": ffn_dw_b.reshape(nl, nc, 1, FFN_CHUNK),
        "w_down": w_down.reshape(nl, nc, FFN_CHUNK, d).astype(_BF16),
    }
    w_in_b = w_in.astype(_BF16)
    b_in_v, q_gain_v, k_gain_v = vec(b_in), vec(q_gain), vec(k_gain)
    cos_t, sin_t = _rope_tables(n)
    ones_t = jnp.ones((n_ctx, HEAD_DIM), _F32)

    t_lat = _pick_tile(n, 512)
    t_ctx = _pick_tile(n_ctx, 512)

    for l in range(nl):
        last = l == nl - 1
        inproj = functools.partial(_inproj, l=l, w_in=w_in_b, b_in=b_in_v, q_gain=q_gain_v,
                                   k_gain=k_gain_v)
        q_l, k_l, v_l, a_l, p_l, g_l = inproj(x, mod_lat, True, cos_t=cos_t, sin_t=sin_t,
                                              use_rope=True, kv_only=False, tile=t_lat)
        if last:
            k_c, v_c = inproj(ctx, mod_ctx, False, cos_t=ones_t, sin_t=ones_t,
                              use_rope=False, kv_only=True, tile=t_ctx)
        else:
            q_c, k_c, v_c, a_c, p_c, g_c = inproj(ctx, mod_ctx, False, cos_t=ones_t, sin_t=ones_t,
                                                  use_rope=False, kv_only=False, tile=t_ctx)
        attn_l = _attention(q_l, [(k_c, v_c), (k_l, v_l)], tile=t_lat)
        x = _mix(x, attn_l, a_l, p_l, g_l, mod_lat, True, l, wts, tile=t_lat)
        x = _ffn(x, mod_lat, True, l, wts, tile=t_lat)
        if not last:
            attn_c = _attention(q_c, [(k_c, v_c)], tile=t_ctx)
            ctx = _mix(ctx, attn_c, a_c, p_c, g_c, mod_ctx, False, l, wts, tile=t_ctx)
            ctx = _ffn(ctx, mod_ctx, False, l, wts, tile=t_ctx)
    return x
```

```python
import functools

import jax
import jax.numpy as jnp
from jax import lax
from jax.experimental import pallas as pl
from jax.experimental.pallas import tpu as pltpu

_F32 = jnp.float32
_BF16 = jnp.bfloat16

D_MODEL = 1024
HEAD_DIM = 128
N_HEADS = 8
N_KV_HEADS = 2
Q_GROUP = N_HEADS // N_KV_HEADS
GRID_W = 64
ROPE_THETA = 10000.0
ROPE_PAIRS = HEAD_DIM // 4
ATTN_SCALE = HEAD_DIM ** -0.5
LOG2_E = 1.4426950408889634
Q_SCALE = ATTN_SCALE * LOG2_E
CONV_WIDTH = 31
POOL_WINDOWS = (2, 4, 8, 16)
POOL_GROUP_CH = D_MODEL // len(POOL_WINDOWS)
D_FF = 2816
N_MOD = 6
ATTN_W = N_HEADS * HEAD_DIM
KV_W = N_KV_HEADS * HEAD_DIM
Q_END = ATTN_W
K_END = Q_END + KV_W
V_END = K_END + KV_W
CONV_END = V_END + 2 * D_MODEL
POOL_END = CONV_END + D_MODEL
D_IN = POOL_END + 3 * D_MODEL
DEEPNORM_ALPHA = (2 * 4) ** 0.25
LN_EPS = 1e-5
RMS_EPS = 1e-6

SUBLANES = 8
BF16_ROWS = 16
VMEM_LIMIT_BYTES = 58 * 1024 * 1024

HALO = BF16_ROWS
FFN_HALO = SUBLANES
FFN_CHUNK = 256
CONV_ROWS = 32
CONV_LANES = 128


def _sigmoid(v):
    return 1.0 / (1.0 + jnp.exp(-v))


def _layer_norm(t, g, b):
    mu = jnp.mean(t, axis=-1, keepdims=True)
    tc = t - mu
    var = jnp.mean(tc * tc, axis=-1, keepdims=True)
    return tc * lax.rsqrt(var + LN_EPS) * g + b


def _resident(block_shape, index):
    return pl.BlockSpec(block_shape, lambda *_: index, pipeline_mode=pl.Buffered(1))


def _params(n_grid):
    return pltpu.CompilerParams(dimension_semantics=("arbitrary",) * n_grid,
                                vmem_limit_bytes=VMEM_LIMIT_BYTES)


def _ada_kernel(cond_ref, w_ref, b_ref, o_ref):
    cnd = cond_ref[...]
    act = (cnd * _sigmoid(cnd)).astype(_BF16)
    o_ref[...] = jnp.dot(act, w_ref[...].astype(_BF16), preferred_element_type=_F32) + b_ref[...]


def _ada_all_layers(cond, w_ada, b_ada):
    nl, d, n = w_ada.shape
    r = cond.shape[0]
    nb = 1536
    return pl.pallas_call(
        _ada_kernel,
        grid=(nl, n // nb),
        in_specs=[pl.BlockSpec((r, d), lambda l, j: (0, 0)),
                  pl.BlockSpec((None, d, nb), lambda l, j: (l, 0, j)),
                  pl.BlockSpec((None, 1, nb), lambda l, j: (l, 0, j))],
        out_specs=pl.BlockSpec((None, r, nb), lambda l, j: (l, 0, j)),
        out_shape=jax.ShapeDtypeStruct((nl, r, n), _F32),
        compiler_params=_params(2),
        name="ada_mod",
    )(cond, w_ada, b_ada.reshape(nl, 1, n))


def _inproj_kernel(x_ref, mod_ref, w_ref, b_ref, qg_ref, kg_ref, cos_ref, sin_ref, *out_refs,
                   use_rope, kv_only):
    x = x_ref[...]
    shift = mod_ref[0:1, :]
    scale = mod_ref[1:2, :]
    h = (x * (1.0 + scale) + shift).astype(_BF16)

    def proj(c0, n):
        return (jnp.dot(h, w_ref[:, c0:c0 + n], preferred_element_type=_F32)
                + b_ref[:, c0:c0 + n])

    if use_rope:
        cos = cos_ref[...]
        sin = sin_ref[...]
        lane = lax.broadcasted_iota(jnp.int32, cos.shape, 1)
        low_half = (lane & ROPE_PAIRS) == 0

    def norm_rope(t, gain, out_scale):
        ms = jnp.mean(t * t, axis=-1, keepdims=True)
        tn = t * lax.rsqrt(ms + RMS_EPS) * gain
        if use_rope:
            partner = jnp.where(low_half,
                                pltpu.roll(tn, HEAD_DIM - ROPE_PAIRS, 1),
                                pltpu.roll(tn, ROPE_PAIRS, 1))
            tn = tn * cos + partner * sin
        if out_scale != 1.0:
            tn = tn * out_scale
        return tn.astype(_BF16)

    if kv_only:
        k_ref, v_ref = out_refs
    else:
        q_ref, k_ref, v_ref, a_ref, p_ref, g_ref = out_refs
        zq = proj(0, ATTN_W)
        qg = qg_ref[...]
        for hd in range(N_HEADS):
            sl = slice(hd * HEAD_DIM, (hd + 1) * HEAD_DIM)
            q_ref[:, sl] = norm_rope(zq[:, sl], qg, Q_SCALE)

    zkv = proj(Q_END, 2 * KV_W)
    kg = kg_ref[...]
    for hd in range(N_KV_HEADS):
        sl = slice(hd * HEAD_DIM, (hd + 1) * HEAD_DIM)
        k_ref[:, sl] = norm_rope(zkv[:, sl], kg, 1.0)
    v_ref[...] = zkv[:, KV_W:].astype(_BF16)

    if not kv_only:
        half = D_MODEL // 2
        for j in range(2):
            a = proj(V_END + j * half, half)
            gt = proj(V_END + D_MODEL + j * half, half)
            a_ref[:, j * half:(j + 1) * half] = (a * _sigmoid(gt)).astype(_BF16)
        p_ref[...] = proj(CONV_END, D_MODEL).astype(_BF16)
        for j in range(3):
            g_ref[:, j * D_MODEL:(j + 1) * D_MODEL] = _sigmoid(
                proj(POOL_END + j * D_MODEL, D_MODEL)).astype(_BF16)


def _inproj(x, mod, per_batch_mod, l, w_in, b_in, q_gain, k_gain, cos_t, sin_t, *,
            use_rope, kv_only, tile):
    bsz, n, d = x.shape
    nt = n // tile
    mod_idx = (lambda b, i: (l, b, 0, 0)) if per_batch_mod else (lambda b, i: (l, 0, 0, 0))
    tok = lambda w: pl.BlockSpec((None, tile, w), lambda b, i: (b, i, 0))
    widths = (KV_W, KV_W) if kv_only else (ATTN_W, KV_W, KV_W, D_MODEL, D_MODEL, 3 * D_MODEL)
    return pl.pallas_call(
        functools.partial(_inproj_kernel, use_rope=use_rope, kv_only=kv_only),
        grid=(bsz, nt),
        in_specs=[tok(d),
                  pl.BlockSpec((None, None, N_MOD, d), mod_idx),
                  _resident((None, d, D_IN), (l, 0, 0)),
                  _resident((None, 1, D_IN), (l, 0, 0)),
                  _resident((None, 1, HEAD_DIM), (l, 0, 0)),
                  _resident((None, 1, HEAD_DIM), (l, 0, 0)),
                  pl.BlockSpec((tile, HEAD_DIM), lambda b, i: (i, 0)),
                  pl.BlockSpec((tile, HEAD_DIM), lambda b, i: (i, 0))],
        out_specs=[tok(w) for w in widths],
        out_shape=[jax.ShapeDtypeStruct((bsz, n, w), _BF16) for w in widths],
        compiler_params=_params(2),
        name="in_proj",
    )(x, mod, w_in, b_in, q_gain, k_gain, cos_t, sin_t)


def _attn_kernel(q_ref, *refs, n_src):
    kv_refs = refs[:2 * n_src]
    o_ref = refs[2 * n_src]
    vext = refs[2 * n_src + 1:]
    nt_dims = (((1,), (1,)), ((), ()))

    @pl.when(pl.program_id(2) == 0)
    def _():
        for s in range(n_src):
            v = kv_refs[2 * s + 1][...]
            vext[s][:, 0:HEAD_DIM] = v
            vext[s][:, HEAD_DIM:] = jnp.ones(v.shape, v.dtype)

    def qk(g):
        qg = q_ref[:, g * HEAD_DIM:(g + 1) * HEAD_DIM]
        return [lax.dot_general(qg, kv_refs[2 * s][...], nt_dims, preferred_element_type=_F32)
                for s in range(n_src)]

    scores_next = qk(0)
    for g in range(Q_GROUP):
        sl = slice(g * HEAD_DIM, (g + 1) * HEAD_DIM)
        scores = scores_next
        scores_next = qk(g + 1) if g + 1 < Q_GROUP else None
        m = jnp.max(scores[0], axis=-1, keepdims=True)
        for s in scores[1:]:
            m = jnp.maximum(m, jnp.max(s, axis=-1, keepdims=True))
        acc = None
        for s in range(n_src):
            p = jnp.exp2(scores[s] - m).astype(_BF16)
            pv = jnp.dot(p, vext[s][...], preferred_element_type=_F32)
            acc = pv if acc is None else acc + pv
        o_ref[:, sl] = (acc[:, 0:HEAD_DIM] / acc[:, HEAD_DIM:]).astype(_BF16)


def _attention(q, sources, *, tile):
    bsz, n, _ = q.shape
    gw = Q_GROUP * HEAD_DIM
    in_specs = [pl.BlockSpec((None, tile, gw), lambda b, h, i: (b, i, h))]
    args = [q]
    for k, v in sources:
        nk = k.shape[1]
        in_specs += [pl.BlockSpec((None, nk, HEAD_DIM), lambda b, h, i: (b, 0, h))] * 2
        args += [k, v]
    return pl.pallas_call(
        functools.partial(_attn_kernel, n_src=len(sources)),
        grid=(bsz, N_KV_HEADS, n // tile),
        in_specs=in_specs,
        out_specs=pl.BlockSpec((None, tile, gw), lambda b, h, i: (b, i, h)),
        out_shape=jax.ShapeDtypeStruct((bsz, n, ATTN_W), _BF16),
        scratch_shapes=[pltpu.VMEM((k.shape[1], 2 * HEAD_DIM), _BF16) for k, _ in sources],
        compiler_params=_params(3),
        name="attention",
    )(*args)


def _mix_kernel(x_ref, attn_ref, a_ref, ap_ref, an_ref, p_ref, pp_ref, pn_ref, g_ref, mod_ref,
                dww_ref, dwb_ref, clg_ref, clb_ref, pww_ref, pwb_ref, poolw_ref, pscale_ref,
                wout_ref, bout_ref, lg_ref, lb_ref, o_ref,
                abuf, sbuf, cbuf, pbuf, l1, l2, l3, l4, *, tile, seq_len):
    i = pl.program_id(1)
    nt = pl.num_programs(1)
    has_prev = i > 0
    has_next = i < nt - 1
    t = tile

    abuf[0:HALO, :] = jnp.where(has_prev, ap_ref[...].astype(_F32), 0.0)
    abuf[HALO:HALO + t, :] = a_ref[...].astype(_F32)
    abuf[HALO + t:2 * HALO + t, :] = jnp.where(has_next, an_ref[...].astype(_F32), 0.0)
    pbuf[0:HALO, :] = jnp.where(has_prev, pp_ref[...].astype(_F32), 0.0)
    pbuf[HALO:HALO + t, :] = p_ref[...].astype(_F32)
    pbuf[HALO + t:2 * HALO + t, :] = jnp.where(has_next, pn_ref[...].astype(_F32), 0.0)
    pbuf[2 * HALO + t:, :] = jnp.zeros((SUBLANES, D_MODEL), _F32)

    span = t + 3 * SUBLANES
    for lb in range(D_MODEL // CONV_LANES):
        lanes = slice(lb * CONV_LANES, (lb + 1) * CONV_LANES)
        for j in range(1, SUBLANES):
            sbuf[j] = abuf[j:j + span, lanes]

        def conv_rows(it, carry, lanes=lanes):
            base = pl.multiple_of(it * CONV_ROWS, CONV_ROWS)
            acc = jnp.zeros((CONV_ROWS, CONV_LANES), _F32)
            for k in range(CONV_WIDTH):
                o = k + 1
                hi, lo = o // SUBLANES, o % SUBLANES
                if lo == 0:
                    win = abuf[pl.ds(base + o, CONV_ROWS), lanes]
                else:
                    win = sbuf[lo, pl.ds(base + hi * SUBLANES, CONV_ROWS), :]
                acc = acc + win * dww_ref[k:k + 1, lanes]
            cbuf[pl.ds(base, CONV_ROWS), lanes] = acc + dwb_ref[:, lanes]
            return carry

        lax.fori_loop(0, t // CONV_ROWS, conv_rows, 0)

    hn = _layer_norm(cbuf[...], clg_ref[...], clb_ref[...])
    hs = (hn * _sigmoid(hn)).astype(_BF16)
    conv_o = jnp.dot(hs, pww_ref[...], preferred_element_type=_F32) + pwb_ref[...]

    gc = POOL_GROUP_CH
    n1, n2, n3, n4 = t + 4 * SUBLANES, t + 3 * SUBLANES, t + 2 * SUBLANES, t + SUBLANES
    l1[...] = pbuf[0:n1, :] + pbuf[1:n1 + 1, :]
    l2[...] = l1[0:n2, gc:] + l1[2:n2 + 2, gc:]
    l3[...] = l2[0:n3, gc:] + l2[4:n3 + 4, gc:]
    l4[...] = l3[0:n4, gc:] + l3[8:n4 + 8, gc:]
    win_sums = (l1[HALO - 1:HALO - 1 + t, 0:gc],
                l2[HALO - 2:HALO - 2 + t, 0:gc],
                l3[HALO - 4:HALO - 4 + t, 0:gc],
                l4[HALO - 8:HALO - 8 + t, :])
    pos = i * t + lax.broadcasted_iota(jnp.int32, (t, 1), 0)
    mixed = []
    for g, w in enumerate(POOL_WINDOWS):
        lo = jnp.clip(pos - w // 2, 0, seq_len)
        hi = jnp.clip(pos - w // 2 + w, 0, seq_len)
        cnt = (hi - lo).astype(_F32)
        pooled = win_sums[g] / cnt - pbuf[HALO:HALO + t, g * gc:(g + 1) * gc]
        mixed.append(jnp.dot(pooled.astype(_BF16), poolw_ref[g], preferred_element_type=_F32))
    pool_o = jnp.concatenate(mixed, axis=-1) * pscale_ref[...]

    m = (g_ref[:, 0:D_MODEL].astype(_F32) * attn_ref[...].astype(_F32)
         + g_ref[:, D_MODEL:2 * D_MODEL].astype(_F32) * conv_o
         + g_ref[:, 2 * D_MODEL:].astype(_F32) * pool_o)
    y = jnp.dot(m.astype(_BF16), wout_ref[...], preferred_element_type=_F32) + bout_ref[...]
    gate = mod_ref[2:3, :]
    o_ref[...] = _layer_norm(DEEPNORM_ALPHA * x_ref[...] + gate * y, lg_ref[...], lb_ref[...])


def _mix(x, attn, a, p, g, mod, per_batch_mod, l, wts, *, tile):
    bsz, n, d = x.shape
    nt = n // tile
    hb = tile // HALO
    last_hb = n // HALO - 1
    mod_idx = (lambda b, i: (l, b, 0, 0)) if per_batch_mod else (lambda b, i: (l, 0, 0, 0))
    tok = lambda w: pl.BlockSpec((None, tile, w), lambda b, i: (b, i, 0))
    prev = pl.BlockSpec((None, HALO, d), lambda b, i: (b, jnp.maximum(i * hb - 1, 0), 0))
    nxt = pl.BlockSpec((None, HALO, d), lambda b, i: (b, jnp.minimum((i + 1) * hb, last_hb), 0))
    vec = _resident((None, 1, d), (l, 0, 0))
    mat = _resident((None, d, d), (l, 0, 0))
    gc = POOL_GROUP_CH
    return pl.pallas_call(
        functools.partial(_mix_kernel, tile=tile, seq_len=n),
        grid=(bsz, nt),
        in_specs=[tok(d), tok(d), tok(d), prev, nxt, tok(d), prev, nxt, tok(3 * d),
                  pl.BlockSpec((None, None, N_MOD, d), mod_idx),
                  _resident((None, CONV_WIDTH, d), (l, 0, 0)), vec, vec, vec, mat, vec,
                  _resident((None, len(POOL_WINDOWS), gc, gc), (l, 0, 0, 0)), vec,
                  mat, vec, vec, vec],
        out_specs=tok(d),
        out_shape=jax.ShapeDtypeStruct((bsz, n, d), _F32),
        scratch_shapes=[
            pltpu.VMEM((tile + 2 * HALO, d), _F32),
            pltpu.VMEM((SUBLANES, tile + 3 * SUBLANES, CONV_LANES), _F32),
            pltpu.VMEM((tile, d), _F32),
            pltpu.VMEM((tile + 2 * HALO + SUBLANES, d), _F32),
            pltpu.VMEM((tile + 4 * SUBLANES, d), _F32),
            pltpu.VMEM((tile + 3 * SUBLANES, d - gc), _F32),
            pltpu.VMEM((tile + 2 * SUBLANES, d - 2 * gc), _F32),
            pltpu.VMEM((tile + SUBLANES, d - 3 * gc), _F32),
        ],
        compiler_params=_params(2),
        name="mix",
    )(x, attn, a, a, a, p, p, p, g, mod,
      wts["conv_dw_w"], wts["conv_dw_b"], wts["conv_ln_g"], wts["conv_ln_b"],
      wts["conv_pw_w"], wts["conv_pw_b"], wts["pool_w"], wts["pool_scale"],
      wts["w_out"], wts["b_out"], wts["ln1_g"], wts["ln1_b"])


def _ffn_kernel(x_ref, xp_ref, xn_ref, mod_ref, wup_ref, dww_ref, dwb_ref, wdn_ref, lg_ref, lb_ref,
                o_ref, hext, hb, abuf, acc, *, tile):
    i = pl.program_id(1)
    nt = pl.num_programs(1)
    t = tile
    shift = mod_ref[3:4, :]
    scale = mod_ref[4:5, :]
    gate = mod_ref[5:6, :]

    def modulate(v):
        return v * (1.0 + scale) + shift

    hext[0:FFN_HALO, :] = jnp.where(i > 0, modulate(xp_ref[...]), 0.0)
    hext[FFN_HALO:FFN_HALO + t, :] = modulate(x_ref[...])
    hext[FFN_HALO + t:, :] = jnp.where(i < nt - 1, modulate(xn_ref[...]), 0.0)
    hb[...] = hext[...].astype(_BF16)
    acc[...] = jnp.zeros(acc.shape, _F32)

    def up(c):
        return jnp.dot(hb[...], wup_ref[c], preferred_element_type=_F32)

    nc = D_FF // FFN_CHUNK
    z = up(0)
    for c in range(nc):
        z_next = up(c + 1) if c + 1 < nc else None
        ab = abuf.at[c % 2]
        ab[...] = z[:, 0:FFN_CHUNK]
        dw = dww_ref[c]
        y = (ab[FFN_HALO - 1:FFN_HALO - 1 + t, :] * dw[0:1, :]
             + ab[FFN_HALO:FFN_HALO + t, :] * dw[1:2, :]
             + ab[FFN_HALO + 1:FFN_HALO + 1 + t, :] * dw[2:3, :]
             + dwb_ref[c])
        u = z[FFN_HALO:FFN_HALO + t, FFN_CHUNK:]
        gact = (y * _sigmoid(y) * u).astype(_BF16)
        acc[...] += jnp.dot(gact, wdn_ref[c], preferred_element_type=_F32)
        z = z_next
    o_ref[...] = _layer_norm(DEEPNORM_ALPHA * x_ref[...] + gate * acc[...], lg_ref[...], lb_ref[...])


def _ffn(x, mod, per_batch_mod, l, wts, *, tile):
    bsz, n, d = x.shape
    nt = n // tile
    hb = tile // FFN_HALO
    last_hb = n // FFN_HALO - 1
    nc = D_FF // FFN_CHUNK
    mod_idx = (lambda b, i: (l, b, 0, 0)) if per_batch_mod else (lambda b, i: (l, 0, 0, 0))
    tok = pl.BlockSpec((None, tile, d), lambda b, i: (b, i, 0))
    prev = pl.BlockSpec((None, FFN_HALO, d), lambda b, i: (b, jnp.maximum(i * hb - 1, 0), 0))
    nxt = pl.BlockSpec((None, FFN_HALO, d), lambda b, i: (b, jnp.minimum((i + 1) * hb, last_hb), 0))
    vec = _resident((None, 1, d), (l, 0, 0))
    return pl.pallas_call(
        functools.partial(_ffn_kernel, tile=tile),
        grid=(bsz, nt),
        in_specs=[tok, prev, nxt,
                  pl.BlockSpec((None, None, N_MOD, d), mod_idx),
                  _resident((None, nc, d, 2 * FFN_CHUNK), (l, 0, 0, 0)),
                  _resident((None, nc, 3, FFN_CHUNK), (l, 0, 0, 0)),
                  _resident((None, nc, 1, FFN_CHUNK), (l, 0, 0, 0)),
                  _resident((None, nc, FFN_CHUNK, d), (l, 0, 0, 0)),
                  vec, vec],
        out_specs=tok,
        out_shape=jax.ShapeDtypeStruct((bsz, n, d), _F32),
        scratch_shapes=[
            pltpu.VMEM((tile + 2 * FFN_HALO, d), _F32),
            pltpu.VMEM((tile + 2 * FFN_HALO, d), _BF16),
            pltpu.VMEM((2, tile + 2 * FFN_HALO, FFN_CHUNK), _F32),
            pltpu.VMEM((tile, d), _F32),
        ],
        compiler_params=_params(2),
        name="conv_ffn",
    )(x, x, x, mod, wts["w_up"], wts["ffn_dw_w"], wts["ffn_dw_b"], wts["w_down"],
      wts["ln2_g"], wts["ln2_b"])


def _rope_tables(n_tokens):
    n_rows = n_tokens // GRID_W
    row = jnp.repeat(jnp.arange(n_rows, dtype=_F32), GRID_W)
    col = jnp.tile(jnp.arange(GRID_W, dtype=_F32), n_rows)
    inv_freq = ROPE_THETA ** (-jnp.arange(ROPE_PAIRS, dtype=_F32) / ROPE_PAIRS)
    ang = jnp.stack([row, col], axis=-1)[..., None] * inv_freq
    cos, sin = jnp.cos(ang), jnp.sin(ang)
    cos_t = jnp.concatenate([cos[:, 0], cos[:, 0], cos[:, 1], cos[:, 1]], axis=-1)
    sin_t = jnp.concatenate([-sin[:, 0], sin[:, 0], -sin[:, 1], sin[:, 1]], axis=-1)
    return cos_t, sin_t


def _pick_tile(n, target):
    return target if n % target == 0 else n


def kernel(x, c, ctx, c_ctx, w_ada, b_ada, w_in, b_in, q_gain, k_gain, conv_dw_w, conv_dw_b,
           conv_ln_g, conv_ln_b, conv_pw_w, conv_pw_b, pool_w, pool_scale, w_out, b_out,
           ln1_g, ln1_b, ln2_g, ln2_b, w_up, ffn_dw_w, ffn_dw_b, w_down):
    bsz, n, d = x.shape
    n_ctx = ctx.shape[1]
    nl = w_in.shape[0]
    nc = D_FF // FFN_CHUNK
    assert d == D_MODEL and n % GRID_W == 0 and n % CONV_ROWS == 0 and n_ctx % CONV_ROWS == 0

    n_cond = -(-(bsz + 1) // SUBLANES) * SUBLANES
    cond = jnp.zeros((n_cond, d), _F32).at[:bsz].set(c).at[bsz].set(c_ctx)
    mods = _ada_all_layers(cond, w_ada, b_ada)
    mod_lat = mods[:, :bsz].reshape(nl, bsz, N_MOD, d)
    mod_ctx = mods[:, bsz:bsz + 1].reshape(nl, 1, N_MOD, d)

    vec = lambda v: v.reshape(nl, 1, v.shape[-1])
    w_up_a = w_up[:, :, :D_FF].reshape(nl, d, nc, FFN_CHUNK)
    w_up_u = w_up[:, :, D_FF:].reshape(nl, d, nc, FFN_CHUNK)
    wts = {
        "conv_dw_w": conv_dw_w, "conv_dw_b": vec(conv_dw_b),
        "conv_ln_g": vec(conv_ln_g), "conv_ln_b": vec(conv_ln_b),
        "conv_pw_w": conv_pw_w.astype(_BF16), "conv_pw_b": vec(conv_pw_b),
        "pool_w": pool_w.astype(_BF16), "pool_scale": vec(pool_scale),
        "w_out": w_out.astype(_BF16), "b_out": vec(b_out),
        "ln1_g": vec(ln1_g), "ln1_b": vec(ln1_b), "ln2_g": vec(ln2_g), "ln2_b": vec(ln2_b),
        "w_up": jnp.concatenate([w_up_a, w_up_u], axis=-1).transpose(0, 2, 1, 3).astype(_BF16),
        "ffn_dw_w": ffn_dw_w.reshape(nl, 3, nc, FFN_CHUNK).transpose(0, 2, 1, 3),
        "ffn_dw_b": ffn_dw_b.reshape(nl, nc, 1, FFN_CHUNK),
        "w_down": w_down.reshape(nl, nc, FFN_CHUNK, d).astype(_BF16),
    }
    w_in_b = w_in.astype(_BF16)
    b_in_v, q_gain_v, k_gain_v = vec(b_in), vec(q_gain), vec(k_gain)
    cos_t, sin_t = _rope_tables(n)
    ones_t = jnp.ones((n_ctx, HEAD_DIM), _F32)

    t_lat = _pick_tile(n, 512)
    t_ctx = _pick_tile(n_ctx, 512)
    t_big = _pick_tile(n, 1024)

    for l in range(nl):
        last = l == nl - 1
        inproj = functools.partial(_inproj, l=l, w_in=w_in_b, b_in=b_in_v, q_gain=q_gain_v,
                                   k_gain=k_gain_v)
        q_l, k_l, v_l, a_l, p_l, g_l = inproj(x, mod_lat, True, cos_t=cos_t, sin_t=sin_t,
                                              use_rope=True, kv_only=False, tile=t_lat)
        if last:
            k_c, v_c = inproj(ctx, mod_ctx, False, cos_t=ones_t, sin_t=ones_t,
                              use_rope=False, kv_only=True, tile=t_ctx)
        else:
            q_c, k_c, v_c, a_c, p_c, g_c = inproj(ctx, mod_ctx, False, cos_t=ones_t, sin_t=ones_t,
                                                  use_rope=False, kv_only=False, tile=t_ctx)
        attn_l = _attention(q_l, [(k_c, v_c), (k_l, v_l)], tile=t_big)
        x = _mix(x, attn_l, a_l, p_l, g_l, mod_lat, True, l, wts, tile=t_lat)
        x = _ffn(x, mod_lat, True, l, wts, tile=t_big)
        if not last:
            attn_c = _attention(q_c, [(k_c, v_c)], tile=t_ctx)
            ctx = _mix(ctx, attn_c, a_c, p_c, g_c, mod_ctx, False, l, wts, tile=t_ctx)
            ctx = _ffn(ctx, mod_ctx, False, l, wts, tile=t_ctx)
    return x
```

```python
import functools

import jax
import jax.numpy as jnp
from jax import lax
from jax.experimental import pallas as pl
from jax.experimental.pallas import tpu as pltpu

_F32 = jnp.float32
_BF16 = jnp.bfloat16

D_MODEL = 1024
HEAD_DIM = 128
N_HEADS = 8
N_KV_HEADS = 2
Q_GROUP = N_HEADS // N_KV_HEADS
GRID_W = 64
ROPE_THETA = 10000.0
ROPE_PAIRS = HEAD_DIM // 4
ATTN_SCALE = HEAD_DIM ** -0.5
LOG2_E = 1.4426950408889634
Q_SCALE = ATTN_SCALE * LOG2_E
CONV_WIDTH = 31
POOL_WINDOWS = (2, 4, 8, 16)
POOL_GROUP_CH = D_MODEL // len(POOL_WINDOWS)
D_FF = 2816
N_MOD = 6
ATTN_W = N_HEADS * HEAD_DIM
KV_W = N_KV_HEADS * HEAD_DIM
Q_END = ATTN_W
K_END = Q_END + KV_W
V_END = K_END + KV_W
CONV_END = V_END + 2 * D_MODEL
POOL_END = CONV_END + D_MODEL
D_IN = POOL_END + 3 * D_MODEL
DEEPNORM_ALPHA = (2 * 4) ** 0.25
LN_EPS = 1e-5
RMS_EPS = 1e-6

SUBLANES = 8
BF16_ROWS = 16
VMEM_LIMIT_BYTES = 58 * 1024 * 1024

HALO = BF16_ROWS
FFN_HALO = SUBLANES
FFN_CHUNK = 256
CONV_ROWS = 32
CONV_LANES = 128


def _sigmoid(v):
    return 1.0 / (1.0 + jnp.exp(-v))


def _layer_norm(t, g, b):
    mu = jnp.mean(t, axis=-1, keepdims=True)
    tc = t - mu
    var = jnp.mean(tc * tc, axis=-1, keepdims=True)
    return tc * lax.rsqrt(var + LN_EPS) * g + b


def _resident(block_shape, index):
    return pl.BlockSpec(block_shape, lambda *_: index, pipeline_mode=pl.Buffered(1))


def _params(n_grid):
    return pltpu.CompilerParams(dimension_semantics=("arbitrary",) * n_grid,
                                vmem_limit_bytes=VMEM_LIMIT_BYTES)


def _ada_kernel(cond_ref, w_ref, b_ref, o_ref):
    cnd = cond_ref[...]
    act = (cnd * _sigmoid(cnd)).astype(_BF16)
    o_ref[...] = jnp.dot(act, w_ref[...].astype(_BF16), preferred_element_type=_F32) + b_ref[...]


def _ada_all_layers(cond, w_ada, b_ada):
    nl, d, n = w_ada.shape
    r = cond.shape[0]
    nb = 1536
    return pl.pallas_call(
        _ada_kernel,
        grid=(nl, n // nb),
        in_specs=[pl.BlockSpec((r, d), lambda l, j: (0, 0)),
                  pl.BlockSpec((None, d, nb), lambda l, j: (l, 0, j)),
                  pl.BlockSpec((None, 1, nb), lambda l, j: (l, 0, j))],
        out_specs=pl.BlockSpec((None, r, nb), lambda l, j: (l, 0, j)),
        out_shape=jax.ShapeDtypeStruct((nl, r, n), _F32),
        compiler_params=_params(2),
        name="ada_mod",
    )(cond, w_ada, b_ada.reshape(nl, 1, n))


def _inproj_kernel(x_ref, mod_ref, w_ref, b_ref, qg_ref, kg_ref, cos_ref, sin_ref, *out_refs,
                   use_rope, kv_only):
    x = x_ref[...]
    shift = mod_ref[0:1, :]
    scale = mod_ref[1:2, :]
    h = (x * (1.0 + scale) + shift).astype(_BF16)

    def proj(c0, n):
        return (jnp.dot(h, w_ref[:, c0:c0 + n], preferred_element_type=_F32)
                + b_ref[:, c0:c0 + n])

    if use_rope:
        cos = cos_ref[...]
        sin = sin_ref[...]
        lane = lax.broadcasted_iota(jnp.int32, cos.shape, 1)
        low_half = (lane & ROPE_PAIRS) == 0

    def norm_rope(t, gain, out_scale):
        ms = jnp.mean(t * t, axis=-1, keepdims=True)
        tn = t * lax.rsqrt(ms + RMS_EPS) * gain
        if use_rope:
            partner = jnp.where(low_half,
                                pltpu.roll(tn, HEAD_DIM - ROPE_PAIRS, 1),
                                pltpu.roll(tn, ROPE_PAIRS, 1))
            tn = tn * cos + partner * sin
        if out_scale != 1.0:
            tn = tn * out_scale
        return tn.astype(_BF16)

    if kv_only:
        k_ref, v_ref = out_refs
    else:
        q_ref, k_ref, v_ref, a_ref, p_ref, g_ref = out_refs
        zq = proj(0, ATTN_W)
        qg = qg_ref[...]
        for hd in range(N_HEADS):
            sl = slice(hd * HEAD_DIM, (hd + 1) * HEAD_DIM)
            q_ref[:, sl] = norm_rope(zq[:, sl], qg, Q_SCALE)

    zkv = proj(Q_END, 2 * KV_W)
    kg = kg_ref[...]
    for hd in range(N_KV_HEADS):
        sl = slice(hd * HEAD_DIM, (hd + 1) * HEAD_DIM)
        k_ref[:, sl] = norm_rope(zkv[:, sl], kg, 1.0)
    v_ref[...] = zkv[:, KV_W:].astype(_BF16)

    if not kv_only:
        half = D_MODEL // 2
        for j in range(2):
            a = proj(V_END + j * half, half)
            gt = proj(V_END + D_MODEL + j * half, half)
            a_ref[:, j * half:(j + 1) * half] = (a * _sigmoid(gt)).astype(_BF16)
        p_ref[...] = proj(CONV_END, D_MODEL).astype(_BF16)
        for j in range(3):
            g_ref[:, j * D_MODEL:(j + 1) * D_MODEL] = _sigmoid(
                proj(POOL_END + j * D_MODEL, D_MODEL)).astype(_BF16)


def _inproj(x, mod, per_batch_mod, l, w_in, b_in, q_gain, k_gain, cos_t, sin_t, *,
            use_rope, kv_only, tile):
    bsz, n, d = x.shape
    nt = n // tile
    mod_idx = (lambda b, i: (l, b, 0, 0)) if per_batch_mod else (lambda b, i: (l, 0, 0, 0))
    tok = lambda w: pl.BlockSpec((None, tile, w), lambda b, i: (b, i, 0))
    widths = (KV_W, KV_W) if kv_only else (ATTN_W, KV_W, KV_W, D_MODEL, D_MODEL, 3 * D_MODEL)
    return pl.pallas_call(
        functools.partial(_inproj_kernel, use_rope=use_rope, kv_only=kv_only),
        grid=(bsz, nt),
        in_specs=[tok(d),
                  pl.BlockSpec((None, None, N_MOD, d), mod_idx),
                  _resident((None, d, D_IN), (l, 0, 0)),
                  _resident((None, 1, D_IN), (l, 0, 0)),
                  _resident((None, 1, HEAD_DIM), (l, 0, 0)),
                  _resident((None, 1, HEAD_DIM), (l, 0, 0)),
                  pl.BlockSpec((tile, HEAD_DIM), lambda b, i: (i, 0)),
                  pl.BlockSpec((tile, HEAD_DIM), lambda b, i: (i, 0))],
        out_specs=[tok(w) for w in widths],
        out_shape=[jax.ShapeDtypeStruct((bsz, n, w), _BF16) for w in widths],
        compiler_params=_params(2),
        name="in_proj",
    )(x, mod, w_in, b_in, q_gain, k_gain, cos_t, sin_t)


def _attn_kernel(q_ref, *refs, n_src):
    kv_refs = refs[:2 * n_src]
    o_ref = refs[2 * n_src]
    vext = refs[2 * n_src + 1:]
    nt_dims = (((1,), (1,)), ((), ()))

    @pl.when(pl.program_id(2) == 0)
    def _():
        for s in range(n_src):
            v = kv_refs[2 * s + 1][...]
            vext[s][:, 0:HEAD_DIM] = v
            vext[s][:, HEAD_DIM:] = jnp.ones(v.shape, v.dtype)

    def qk(g):
        qg = q_ref[:, g * HEAD_DIM:(g + 1) * HEAD_DIM]
        return [lax.dot_general(qg, kv_refs[2 * s][...], nt_dims, preferred_element_type=_F32)
                for s in range(n_src)]

    scores_next = qk(0)
    for g in range(Q_GROUP):
        sl = slice(g * HEAD_DIM, (g + 1) * HEAD_DIM)
        scores = scores_next
        scores_next = qk(g + 1) if g + 1 < Q_GROUP else None
        m = jnp.max(scores[0], axis=-1, keepdims=True)
        for s in scores[1:]:
            m = jnp.maximum(m, jnp.max(s, axis=-1, keepdims=True))
        acc = None
        for s in range(n_src):
            p = jnp.exp2(scores[s] - m).astype(_BF16)
            pv = jnp.dot(p, vext[s][...], preferred_element_type=_F32)
            acc = pv if acc is None else acc + pv
        o_ref[:, sl] = (acc[:, 0:HEAD_DIM] / acc[:, HEAD_DIM:]).astype(_BF16)


def _attention(q, sources, *, tile):
    bsz, n, _ = q.shape
    gw = Q_GROUP * HEAD_DIM
    in_specs = [pl.BlockSpec((None, tile, gw), lambda b, h, i: (b, i, h))]
    args = [q]
    for k, v in sources:
        nk = k.shape[1]
        in_specs += [pl.BlockSpec((None, nk, HEAD_DIM), lambda b, h, i: (b, 0, h))] * 2
        args += [k, v]
    return pl.pallas_call(
        functools.partial(_attn_kernel, n_src=len(sources)),
        grid=(bsz, N_KV_HEADS, n // tile),
        in_specs=in_specs,
        out_specs=pl.BlockSpec((None, tile, gw), lambda b, h, i: (b, i, h)),
        out_shape=jax.ShapeDtypeStruct((bsz, n, ATTN_W), _BF16),
        scratch_shapes=[pltpu.VMEM((k.shape[1], 2 * HEAD_DIM), _BF16) for k, _ in sources],
        compiler_params=_params(3),
        name="attention",
    )(*args)


def _mix_kernel(x_ref, attn_ref, a_ref, ap_ref, an_ref, p_ref, pp_ref, pn_ref, g_ref, mod_ref,
                dww_ref, dwb_ref, clg_ref, clb_ref, pww_ref, pwb_ref, poolw_ref, pscale_ref,
                wout_ref, bout_ref, lg_ref, lb_ref, o_ref,
                abuf, sbuf, cbuf, pbuf, l1, l2, l3, l4, *, tile, seq_len):
    i = pl.program_id(1)
    nt = pl.num_programs(1)
    has_prev = i > 0
    has_next = i < nt - 1
    t = tile

    abuf[0:HALO, :] = jnp.where(has_prev, ap_ref[...].astype(_F32), 0.0)
    abuf[HALO:HALO + t, :] = a_ref[...].astype(_F32)
    abuf[HALO + t:2 * HALO + t, :] = jnp.where(has_next, an_ref[...].astype(_F32), 0.0)
    pbuf[0:HALO, :] = jnp.where(has_prev, pp_ref[...].astype(_F32), 0.0)
    pbuf[HALO:HALO + t, :] = p_ref[...].astype(_F32)
    pbuf[HALO + t:2 * HALO + t, :] = jnp.where(has_next, pn_ref[...].astype(_F32), 0.0)
    pbuf[2 * HALO + t:, :] = jnp.zeros((SUBLANES, D_MODEL), _F32)

    span = t + 3 * SUBLANES
    for lb in range(D_MODEL // CONV_LANES):
        lanes = slice(lb * CONV_LANES, (lb + 1) * CONV_LANES)
        for j in range(1, SUBLANES):
            sbuf[j] = abuf[j:j + span, lanes]

        def conv_rows(it, carry, lanes=lanes):
            base = pl.multiple_of(it * CONV_ROWS, CONV_ROWS)
            acc = jnp.zeros((CONV_ROWS, CONV_LANES), _F32)
            for k in range(CONV_WIDTH):
                o = k + 1
                hi, lo = o // SUBLANES, o % SUBLANES
                if lo == 0:
                    win = abuf[pl.ds(base + o, CONV_ROWS), lanes]
                else:
                    win = sbuf[lo, pl.ds(base + hi * SUBLANES, CONV_ROWS), :]
                acc = acc + win * dww_ref[k:k + 1, lanes]
            cbuf[pl.ds(base, CONV_ROWS), lanes] = acc + dwb_ref[:, lanes]
            return carry

        lax.fori_loop(0, t // CONV_ROWS, conv_rows, 0)

    hn = _layer_norm(cbuf[...], clg_ref[...], clb_ref[...])
    hs = (hn * _sigmoid(hn)).astype(_BF16)
    conv_o = jnp.dot(hs, pww_ref[...], preferred_element_type=_F32) + pwb_ref[...]

    gc = POOL_GROUP_CH
    n1, n2, n3, n4 = t + 4 * SUBLANES, t + 3 * SUBLANES, t + 2 * SUBLANES, t + SUBLANES
    l1[...] = pbuf[0:n1, :] + pbuf[1:n1 + 1, :]
    l2[...] = l1[0:n2, gc:] + l1[2:n2 + 2, gc:]
    l3[...] = l2[0:n3, gc:] + l2[4:n3 + 4, gc:]
    l4[...] = l3[0:n4, gc:] + l3[8:n4 + 8, gc:]
    win_sums = (l1[HALO - 1:HALO - 1 + t, 0:gc],
                l2[HALO - 2:HALO - 2 + t, 0:gc],
                l3[HALO - 4:HALO - 4 + t, 0:gc],
                l4[HALO - 8:HALO - 8 + t, :])
    pos = i * t + lax.broadcasted_iota(jnp.int32, (t, 1), 0)
    mixed = []
    for g, w in enumerate(POOL_WINDOWS):
        lo = jnp.clip(pos - w // 2, 0, seq_len)
        hi = jnp.clip(pos - w // 2 + w, 0, seq_len)
        cnt = (hi - lo).astype(_F32)
        pooled = win_sums[g] / cnt - pbuf[HALO:HALO + t, g * gc:(g + 1) * gc]
        mixed.append(jnp.dot(pooled.astype(_BF16), poolw_ref[g], preferred_element_type=_F32))
    pool_o = jnp.concatenate(mixed, axis=-1) * pscale_ref[...]

    m = (g_ref[:, 0:D_MODEL].astype(_F32) * attn_ref[...].astype(_F32)
         + g_ref[:, D_MODEL:2 * D_MODEL].astype(_F32) * conv_o
         + g_ref[:, 2 * D_MODEL:].astype(_F32) * pool_o)
    y = jnp.dot(m.astype(_BF16), wout_ref[...], preferred_element_type=_F32) + bout_ref[...]
    gate = mod_ref[2:3, :]
    o_ref[...] = _layer_norm(DEEPNORM_ALPHA * x_ref[...] + gate * y, lg_ref[...], lb_ref[...])


def _mix(x, attn, a, p, g, mod, per_batch_mod, l, wts, *, tile):
    bsz, n, d = x.shape
    nt = n // tile
    hb = tile // HALO
    last_hb = n // HALO - 1
    mod_idx = (lambda b, i: (l, b, 0, 0)) if per_batch_mod else (lambda b, i: (l, 0, 0, 0))
    tok = lambda w: pl.BlockSpec((None, tile, w), lambda b, i: (b, i, 0))
    prev = pl.BlockSpec((None, HALO, d), lambda b, i: (b, jnp.maximum(i * hb - 1, 0), 0))
    nxt = pl.BlockSpec((None, HALO, d), lambda b, i: (b, jnp.minimum((i + 1) * hb, last_hb), 0))
    vec = _resident((None, 1, d), (l, 0, 0))
    mat = _resident((None, d, d), (l, 0, 0))
    gc = POOL_GROUP_CH
    return pl.pallas_call(
        functools.partial(_mix_kernel, tile=tile, seq_len=n),
        grid=(bsz, nt),
        in_specs=[tok(d), tok(d), tok(d), prev, nxt, tok(d), prev, nxt, tok(3 * d),
                  pl.BlockSpec((None, None, N_MOD, d), mod_idx),
                  _resident((None, CONV_WIDTH, d), (l, 0, 0)), vec, vec, vec, mat, vec,
                  _resident((None, len(POOL_WINDOWS), gc, gc), (l, 0, 0, 0)), vec,
                  mat, vec, vec, vec],
        out_specs=tok(d),
        out_shape=jax.ShapeDtypeStruct((bsz, n, d), _F32),
        scratch_shapes=[
            pltpu.VMEM((tile + 2 * HALO, d), _F32),
            pltpu.VMEM((SUBLANES, tile + 3 * SUBLANES, CONV_LANES), _F32),
            pltpu.VMEM((tile, d), _F32),
            pltpu.VMEM((tile + 2 * HALO + SUBLANES, d), _F32),
            pltpu.VMEM((tile + 4 * SUBLANES, d), _F32),
            pltpu.VMEM((tile + 3 * SUBLANES, d - gc), _F32),
            pltpu.VMEM((tile + 2 * SUBLANES, d - 2 * gc), _F32),
            pltpu.VMEM((tile + SUBLANES, d - 3 * gc), _F32),
        ],
        compiler_params=_params(2),
        name="mix",
    )(x, attn, a, a, a, p, p, p, g, mod,
      wts["conv_dw_w"], wts["conv_dw_b"], wts["conv_ln_g"], wts["conv_ln_b"],
      wts["conv_pw_w"], wts["conv_pw_b"], wts["pool_w"], wts["pool_scale"],
      wts["w_out"], wts["b_out"], wts["ln1_g"], wts["ln1_b"])


def _ffn_kernel(x_ref, xp_ref, xn_ref, mod_ref, wup_ref, dww_ref, dwb_ref, wdn_ref, lg_ref, lb_ref,
                o_ref, hext, hb, abuf, gbuf, *, tile):
    i = pl.program_id(1)
    nt = pl.num_programs(1)
    t = tile
    shift = mod_ref[3:4, :]
    scale = mod_ref[4:5, :]
    gate = mod_ref[5:6, :]

    def modulate(v):
        return v * (1.0 + scale) + shift

    hext[0:FFN_HALO, :] = jnp.where(i > 0, modulate(xp_ref[...]), 0.0)
    hext[FFN_HALO:FFN_HALO + t, :] = modulate(x_ref[...])
    hext[FFN_HALO + t:, :] = jnp.where(i < nt - 1, modulate(xn_ref[...]), 0.0)
    hb[...] = hext[...].astype(_BF16)

    def up(c):
        return jnp.dot(hb[...], wup_ref[c], preferred_element_type=_F32)

    nc = D_FF // FFN_CHUNK
    z = up(0)
    for c in range(nc):
        z_next = up(c + 1) if c + 1 < nc else None
        ab = abuf.at[c % 2]
        ab[...] = z[:, 0:FFN_CHUNK]
        dw = dww_ref[c]
        y = (ab[FFN_HALO - 1:FFN_HALO - 1 + t, :] * dw[0:1, :]
             + ab[FFN_HALO:FFN_HALO + t, :] * dw[1:2, :]
             + ab[FFN_HALO + 1:FFN_HALO + 1 + t, :] * dw[2:3, :]
             + dwb_ref[c])
        u = z[FFN_HALO:FFN_HALO + t, FFN_CHUNK:]
        gbuf[:, c * FFN_CHUNK:(c + 1) * FFN_CHUNK] = (y * _sigmoid(y) * u).astype(_BF16)
        z = z_next
    f = jnp.dot(gbuf[...], wdn_ref[...], preferred_element_type=_F32)
    o_ref[...] = _layer_norm(DEEPNORM_ALPHA * x_ref[...] + gate * f, lg_ref[...], lb_ref[...])


def _ffn(x, mod, per_batch_mod, l, wts, *, tile):
    bsz, n, d = x.shape
    nt = n // tile
    hb = tile // FFN_HALO
    last_hb = n // FFN_HALO - 1
    nc = D_FF // FFN_CHUNK
    mod_idx = (lambda b, i: (l, b, 0, 0)) if per_batch_mod else (lambda b, i: (l, 0, 0, 0))
    tok = pl.BlockSpec((None, tile, d), lambda b, i: (b, i, 0))
    prev = pl.BlockSpec((None, FFN_HALO, d), lambda b, i: (b, jnp.maximum(i * hb - 1, 0), 0))
    nxt = pl.BlockSpec((None, FFN_HALO, d), lambda b, i: (b, jnp.minimum((i + 1) * hb, last_hb), 0))
    vec = _resident((None, 1, d), (l, 0, 0))
    return pl.pallas_call(
        functools.partial(_ffn_kernel, tile=tile),
        grid=(bsz, nt),
        in_specs=[tok, prev, nxt,
                  pl.BlockSpec((None, None, N_MOD, d), mod_idx),
                  _resident((None, nc, d, 2 * FFN_CHUNK), (l, 0, 0, 0)),
                  _resident((None, nc, 3, FFN_CHUNK), (l, 0, 0, 0)),
                  _resident((None, nc, 1, FFN_CHUNK), (l, 0, 0, 0)),
                  _resident((None, D_FF, d), (l, 0, 0)),
                  vec, vec],
        out_specs=tok,
        out_shape=jax.ShapeDtypeStruct((bsz, n, d), _F32),
        scratch_shapes=[
            pltpu.VMEM((tile + 2 * FFN_HALO, d), _F32),
            pltpu.VMEM((tile + 2 * FFN_HALO, d), _BF16),
            pltpu.VMEM((2, tile + 2 * FFN_HALO, FFN_CHUNK), _F32),
            pltpu.VMEM((tile, D_FF), _BF16),
        ],
        compiler_params=_params(2),
        name="conv_ffn",
    )(x, x, x, mod, wts["w_up"], wts["ffn_dw_w"], wts["ffn_dw_b"], wts["w_down"],
      wts["ln2_g"], wts["ln2_b"])


def _rope_tables(n_tokens):
    n_rows = n_tokens // GRID_W
    row = jnp.repeat(jnp.arange(n_rows, dtype=_F32), GRID_W)
    col = jnp.tile(jnp.arange(GRID_W, dtype=_F32), n_rows)
    inv_freq = ROPE_THETA ** (-jnp.arange(ROPE_PAIRS, dtype=_F32) / ROPE_PAIRS)
    ang = jnp.stack([row, col], axis=-1)[..., None] * inv_freq
    cos, sin = jnp.cos(ang), jnp.sin(ang)
    cos_t = jnp.concatenate([cos[:, 0], cos[:, 0], cos[:, 1], cos[:, 1]], axis=-1)
    sin_t = jnp.concatenate([-sin[:, 0], sin[:, 0], -sin[:, 1], sin[:, 1]], axis=-1)
    return cos_t, sin_t


def _pick_tile(n, target):
    return target if n % target == 0 else n


def kernel(x, c, ctx, c_ctx, w_ada, b_ada, w_in, b_in, q_gain, k_gain, conv_dw_w, conv_dw_b,
           conv_ln_g, conv_ln_b, conv_pw_w, conv_pw_b, pool_w, pool_scale, w_out, b_out,
           ln1_g, ln1_b, ln2_g, ln2_b, w_up, ffn_dw_w, ffn_dw_b, w_down):
    bsz, n, d = x.shape
    n_ctx = ctx.shape[1]
    nl = w_in.shape[0]
    nc = D_FF // FFN_CHUNK
    assert d == D_MODEL and n % GRID_W == 0 and n % CONV_ROWS == 0 and n_ctx % CONV_ROWS == 0

    n_cond = -(-(bsz + 1) // SUBLANES) * SUBLANES
    cond = jnp.zeros((n_cond, d), _F32).at[:bsz].set(c).at[bsz].set(c_ctx)
    mods = _ada_all_layers(cond, w_ada, b_ada)
    mod_lat = mods[:, :bsz].reshape(nl, bsz, N_MOD, d)
    mod_ctx = mods[:, bsz:bsz + 1].reshape(nl, 1, N_MOD, d)

    vec = lambda v: v.reshape(nl, 1, v.shape[-1])
    w_up_a = w_up[:, :, :D_FF].reshape(nl, d, nc, FFN_CHUNK)
    w_up_u = w_up[:, :, D_FF:].reshape(nl, d, nc, FFN_CHUNK)
    wts = {
        "conv_dw_w": conv_dw_w, "conv_dw_b": vec(conv_dw_b),
        "conv_ln_g": vec(conv_ln_g), "conv_ln_b": vec(conv_ln_b),
        "conv_pw_w": conv_pw_w.astype(_BF16), "conv_pw_b": vec(conv_pw_b),
        "pool_w": pool_w.astype(_BF16), "pool_scale": vec(pool_scale),
        "w_out": w_out.astype(_BF16), "b_out": vec(b_out),
        "ln1_g": vec(ln1_g), "ln1_b": vec(ln1_b), "ln2_g": vec(ln2_g), "ln2_b": vec(ln2_b),
        "w_up": jnp.concatenate([w_up_a, w_up_u], axis=-1).transpose(0, 2, 1, 3).astype(_BF16),
        "ffn_dw_w": ffn_dw_w.reshape(nl, 3, nc, FFN_CHUNK).transpose(0, 2, 1, 3),
        "ffn_dw_b": ffn_dw_b.reshape(nl, nc, 1, FFN_CHUNK),
        "w_down": w_down.astype(_BF16),
    }
    w_in_b = w_in.astype(_BF16)
    b_in_v, q_gain_v, k_gain_v = vec(b_in), vec(q_gain), vec(k_gain)
    cos_t, sin_t = _rope_tables(n)
    ones_t = jnp.ones((n_ctx, HEAD_DIM), _F32)

    t_lat = _pick_tile(n, 512)
    t_ctx = _pick_tile(n_ctx, 512)
    t_big = _pick_tile(n, 1024)

    for l in range(nl):
        last = l == nl - 1
        inproj = functools.partial(_inproj, l=l, w_in=w_in_b, b_in=b_in_v, q_gain=q_gain_v,
                                   k_gain=k_gain_v)
        q_l, k_l, v_l, a_l, p_l, g_l = inproj(x, mod_lat, True, cos_t=cos_t, sin_t=sin_t,
                                              use_rope=True, kv_only=False, tile=t_lat)
        if last:
            k_c, v_c = inproj(ctx, mod_ctx, False, cos_t=ones_t, sin_t=ones_t,
                              use_rope=False, kv_only=True, tile=t_ctx)
        else:
            q_c, k_c, v_c, a_c, p_c, g_c = inproj(ctx, mod_ctx, False, cos_t=ones_t, sin_t=ones_t,
                                                  use_rope=False, kv_only=False, tile=t_ctx)
        attn_l = _attention(q_l, [(k_c, v_c), (k_l, v_l)], tile=t_big)
        x = _mix(x, attn_l, a_l, p_l, g_l, mod_lat, True, l, wts, tile=t_lat)
        x = _ffn(x, mod_lat, True, l, wts, tile=t_big)
        if not last:
            attn_c = _attention(q_c, [(k_c, v_c)], tile=t_ctx)
            ctx = _mix(ctx, attn_c, a_c, p_c, g_c, mod_ctx, False, l, wts, tile=t_ctx)
            ctx = _ffn(ctx, mod_ctx, False, l, wts, tile=t_ctx)
    return x
```

```python
import functools

import jax
import jax.numpy as jnp
from jax import lax
from jax.experimental import pallas as pl
from jax.experimental.pallas import tpu as pltpu

_F32 = jnp.float32
_BF16 = jnp.bfloat16

D_MODEL = 1024
HEAD_DIM = 128
N_HEADS = 8
N_KV_HEADS = 2
Q_GROUP = N_HEADS // N_KV_HEADS
GRID_W = 64
ROPE_THETA = 10000.0
ROPE_PAIRS = HEAD_DIM // 4
ATTN_SCALE = HEAD_DIM ** -0.5
LOG2_E = 1.4426950408889634
Q_SCALE = ATTN_SCALE * LOG2_E
CONV_WIDTH = 31
POOL_WINDOWS = (2, 4, 8, 16)
POOL_GROUP_CH = D_MODEL // len(POOL_WINDOWS)
D_FF = 2816
N_MOD = 6
ATTN_W = N_HEADS * HEAD_DIM
KV_W = N_KV_HEADS * HEAD_DIM
Q_END = ATTN_W
K_END = Q_END + KV_W
V_END = K_END + KV_W
CONV_END = V_END + 2 * D_MODEL
POOL_END = CONV_END + D_MODEL
D_IN = POOL_END + 3 * D_MODEL
DEEPNORM_ALPHA = (2 * 4) ** 0.25
LN_EPS = 1e-5
RMS_EPS = 1e-6

SUBLANES = 8
BF16_ROWS = 16
VMEM_LIMIT_BYTES = 58 * 1024 * 1024

HALO = BF16_ROWS
FFN_HALO = SUBLANES
FFN_CHUNK = 256
CONV_ROWS = 32
CONV_LANES = 128
CONV_REACH = (CONV_WIDTH // SUBLANES) * SUBLANES


def _sigmoid(v):
    return 1.0 / (1.0 + jnp.exp(-v))


def _layer_norm(t, g, b):
    mu = jnp.mean(t, axis=-1, keepdims=True)
    tc = t - mu
    var = jnp.mean(tc * tc, axis=-1, keepdims=True)
    return tc * lax.rsqrt(var + LN_EPS) * g + b


def _resident(block_shape, index):
    return pl.BlockSpec(block_shape, lambda *_: index, pipeline_mode=pl.Buffered(1))


def _params(n_grid):
    return pltpu.CompilerParams(dimension_semantics=("arbitrary",) * n_grid,
                                vmem_limit_bytes=VMEM_LIMIT_BYTES)


def _ada_kernel(cond_ref, w_ref, b_ref, o_ref):
    cnd = cond_ref[...]
    act = (cnd * _sigmoid(cnd)).astype(_BF16)
    o_ref[...] = jnp.dot(act, w_ref[...].astype(_BF16), preferred_element_type=_F32) + b_ref[...]


def _ada_all_layers(cond, w_ada, b_ada):
    nl, d, n = w_ada.shape
    r = cond.shape[0]
    nb = 1536
    return pl.pallas_call(
        _ada_kernel,
        grid=(nl, n // nb),
        in_specs=[pl.BlockSpec((r, d), lambda l, j: (0, 0)),
                  pl.BlockSpec((None, d, nb), lambda l, j: (l, 0, j)),
                  pl.BlockSpec((None, 1, nb), lambda l, j: (l, 0, j))],
        out_specs=pl.BlockSpec((None, r, nb), lambda l, j: (l, 0, j)),
        out_shape=jax.ShapeDtypeStruct((nl, r, n), _F32),
        compiler_params=_params(2),
        name="ada_mod",
    )(cond, w_ada, b_ada.reshape(nl, 1, n))


def _inproj_kernel(x_ref, mod_ref, w_ref, b_ref, qg_ref, kg_ref, cos_ref, sin_ref, *out_refs,
                   use_rope, kv_only):
    x = x_ref[...]
    shift = mod_ref[0:1, :]
    scale = mod_ref[1:2, :]
    h = (x * (1.0 + scale) + shift).astype(_BF16)

    def proj(c0, n):
        return (jnp.dot(h, w_ref[:, c0:c0 + n], preferred_element_type=_F32)
                + b_ref[:, c0:c0 + n])

    if use_rope:
        cos = cos_ref[...]
        sin = sin_ref[...]
        lane = lax.broadcasted_iota(jnp.int32, cos.shape, 1)
        low_half = (lane & ROPE_PAIRS) == 0

    def norm_rope(t, gain, out_scale):
        ms = jnp.mean(t * t, axis=-1, keepdims=True)
        tn = t * lax.rsqrt(ms + RMS_EPS) * gain
        if use_rope:
            partner = jnp.where(low_half,
                                pltpu.roll(tn, HEAD_DIM - ROPE_PAIRS, 1),
                                pltpu.roll(tn, ROPE_PAIRS, 1))
            tn = tn * cos + partner * sin
        if out_scale != 1.0:
            tn = tn * out_scale
        return tn.astype(_BF16)

    if kv_only:
        k_ref, v_ref = out_refs
    else:
        q_ref, k_ref, v_ref, a_ref, p_ref, g_ref = out_refs
        zq = proj(0, ATTN_W)
        qg = qg_ref[...]
        for hd in range(N_HEADS):
            sl = slice(hd * HEAD_DIM, (hd + 1) * HEAD_DIM)
            q_ref[:, sl] = norm_rope(zq[:, sl], qg, Q_SCALE)

    zkv = proj(Q_END, 2 * KV_W)
    kg = kg_ref[...]
    for hd in range(N_KV_HEADS):
        sl = slice(hd * HEAD_DIM, (hd + 1) * HEAD_DIM)
        k_ref[:, sl] = norm_rope(zkv[:, sl], kg, 1.0)
    v_ref[...] = zkv[:, KV_W:].astype(_BF16)

    if not kv_only:
        half = D_MODEL // 2
        for j in range(2):
            a = proj(V_END + j * half, half)
            gt = proj(V_END + D_MODEL + j * half, half)
            a_ref[:, j * half:(j + 1) * half] = (a * _sigmoid(gt)).astype(_BF16)
        p_ref[...] = proj(CONV_END, D_MODEL).astype(_BF16)
        for j in range(3):
            g_ref[:, j * D_MODEL:(j + 1) * D_MODEL] = _sigmoid(
                proj(POOL_END + j * D_MODEL, D_MODEL)).astype(_BF16)


def _inproj(x, mod, per_batch_mod, l, w_in, b_in, q_gain, k_gain, cos_t, sin_t, *,
            use_rope, kv_only, tile):
    bsz, n, d = x.shape
    nt = n // tile
    mod_idx = (lambda b, i: (l, b, 0, 0)) if per_batch_mod else (lambda b, i: (l, 0, 0, 0))
    tok = lambda w: pl.BlockSpec((None, tile, w), lambda b, i: (b, i, 0))
    widths = (KV_W, KV_W) if kv_only else (ATTN_W, KV_W, KV_W, D_MODEL, D_MODEL, 3 * D_MODEL)
    return pl.pallas_call(
        functools.partial(_inproj_kernel, use_rope=use_rope, kv_only=kv_only),
        grid=(bsz, nt),
        in_specs=[tok(d),
                  pl.BlockSpec((None, None, N_MOD, d), mod_idx),
                  _resident((None, d, D_IN), (l, 0, 0)),
                  _resident((None, 1, D_IN), (l, 0, 0)),
                  _resident((None, 1, HEAD_DIM), (l, 0, 0)),
                  _resident((None, 1, HEAD_DIM), (l, 0, 0)),
                  pl.BlockSpec((tile, HEAD_DIM), lambda b, i: (i, 0)),
                  pl.BlockSpec((tile, HEAD_DIM), lambda b, i: (i, 0))],
        out_specs=[tok(w) for w in widths],
        out_shape=[jax.ShapeDtypeStruct((bsz, n, w), _BF16) for w in widths],
        compiler_params=_params(2),
        name="in_proj",
    )(x, mod, w_in, b_in, q_gain, k_gain, cos_t, sin_t)


def _attn_kernel(q_ref, *refs, n_src):
    kv_refs = refs[:2 * n_src]
    o_ref = refs[2 * n_src]
    vext = refs[2 * n_src + 1:]
    nt_dims = (((1,), (1,)), ((), ()))

    @pl.when(pl.program_id(2) == 0)
    def _():
        for s in range(n_src):
            v = kv_refs[2 * s + 1][...]
            vext[s][:, 0:HEAD_DIM] = v
            vext[s][:, HEAD_DIM:] = jnp.ones(v.shape, v.dtype)

    def qk(g):
        qg = q_ref[:, g * HEAD_DIM:(g + 1) * HEAD_DIM]
        return [lax.dot_general(qg, kv_refs[2 * s][...], nt_dims, preferred_element_type=_F32)
                for s in range(n_src)]

    def softmax_p(scores):
        m = jnp.max(scores[0], axis=-1, keepdims=True)
        for sc in scores[1:]:
            m = jnp.maximum(m, jnp.max(sc, axis=-1, keepdims=True))
        return [jnp.exp2(sc - m).astype(_BF16) for sc in scores]

    def pv_store(g, probs):
        acc = None
        for s in range(n_src):
            pv = jnp.dot(probs[s], vext[s][...], preferred_element_type=_F32)
            acc = pv if acc is None else acc + pv
        o_ref[:, g * HEAD_DIM:(g + 1) * HEAD_DIM] = (
            acc[:, 0:HEAD_DIM] / acc[:, HEAD_DIM:]).astype(_BF16)

    scores_next = qk(0)
    probs_prev = None
    for g in range(Q_GROUP):
        scores = scores_next
        scores_next = qk(g + 1) if g + 1 < Q_GROUP else None
        if probs_prev is not None:
            pv_store(g - 1, probs_prev)
        probs_prev = softmax_p(scores)
    pv_store(Q_GROUP - 1, probs_prev)


def _attention(q, sources, *, tile):
    bsz, n, _ = q.shape
    gw = Q_GROUP * HEAD_DIM
    in_specs = [pl.BlockSpec((None, tile, gw), lambda b, h, i: (b, i, h))]
    args = [q]
    for k, v in sources:
        nk = k.shape[1]
        in_specs += [pl.BlockSpec((None, nk, HEAD_DIM), lambda b, h, i: (b, 0, h))] * 2
        args += [k, v]
    return pl.pallas_call(
        functools.partial(_attn_kernel, n_src=len(sources)),
        grid=(bsz, N_KV_HEADS, n // tile),
        in_specs=in_specs,
        out_specs=pl.BlockSpec((None, tile, gw), lambda b, h, i: (b, i, h)),
        out_shape=jax.ShapeDtypeStruct((bsz, n, ATTN_W), _BF16),
        scratch_shapes=[pltpu.VMEM((k.shape[1], 2 * HEAD_DIM), _BF16) for k, _ in sources],
        compiler_params=_params(3),
        name="attention",
    )(*args)


def _mix_kernel(x_ref, attn_ref, a_ref, ap_ref, an_ref, p_ref, pp_ref, pn_ref, g_ref, mod_ref,
                dww_ref, dwb_ref, clg_ref, clb_ref, pww_ref, pwb_ref, poolw_ref, pscale_ref,
                wout_ref, bout_ref, lg_ref, lb_ref, o_ref,
                abuf, sbuf, cbuf, pbuf, l1, l2, l3, l4, *, tile, seq_len):
    i = pl.program_id(1)
    nt = pl.num_programs(1)
    has_prev = i > 0
    has_next = i < nt - 1
    t = tile

    abuf[0:HALO, :] = jnp.where(has_prev, ap_ref[...].astype(_F32), 0.0)
    abuf[HALO:HALO + t, :] = a_ref[...].astype(_F32)
    abuf[HALO + t:2 * HALO + t, :] = jnp.where(has_next, an_ref[...].astype(_F32), 0.0)
    pbuf[0:HALO, :] = jnp.where(has_prev, pp_ref[...].astype(_F32), 0.0)
    pbuf[HALO:HALO + t, :] = p_ref[...].astype(_F32)
    pbuf[HALO + t:2 * HALO + t, :] = jnp.where(has_next, pn_ref[...].astype(_F32), 0.0)
    pbuf[2 * HALO + t:, :] = jnp.zeros((SUBLANES, D_MODEL), _F32)

    span = t + 3 * SUBLANES
    for lb in range(D_MODEL // CONV_LANES):
        lanes = slice(lb * CONV_LANES, (lb + 1) * CONV_LANES)
        for j in range(1, SUBLANES):
            sbuf[j] = abuf[j:j + span, lanes]

        def conv_rows(it, carry, lanes=lanes):
            base = pl.multiple_of(it * CONV_ROWS, CONV_ROWS)
            acc = None
            for lo in range(SUBLANES):
                if lo == 0:
                    win = abuf[pl.ds(base, CONV_ROWS + CONV_REACH), lanes]
                else:
                    win = sbuf[lo, pl.ds(base, CONV_ROWS + CONV_REACH), :]
                for hi in range(CONV_REACH // SUBLANES + 1):
                    k = hi * SUBLANES + lo - 1
                    if 0 <= k < CONV_WIDTH:
                        term = win[hi * SUBLANES:hi * SUBLANES + CONV_ROWS] * dww_ref[k:k + 1, lanes]
                        acc = term if acc is None else acc + term
            cbuf[pl.ds(base, CONV_ROWS), lanes] = acc + dwb_ref[:, lanes]
            return carry

        lax.fori_loop(0, t // CONV_ROWS, conv_rows, 0)

    hn = _layer_norm(cbuf[...], clg_ref[...], clb_ref[...])
    hs = (hn * _sigmoid(hn)).astype(_BF16)
    conv_o = jnp.dot(hs, pww_ref[...], preferred_element_type=_F32) + pwb_ref[...]

    gc = POOL_GROUP_CH
    n1, n2, n3, n4 = t + 4 * SUBLANES, t + 3 * SUBLANES, t + 2 * SUBLANES, t + SUBLANES
    l1[...] = pbuf[0:n1, :] + pbuf[1:n1 + 1, :]
    l2[...] = l1[0:n2, gc:] + l1[2:n2 + 2, gc:]
    l3[...] = l2[0:n3, gc:] + l2[4:n3 + 4, gc:]
    l4[...] = l3[0:n4, gc:] + l3[8:n4 + 8, gc:]
    win_sums = (l1[HALO - 1:HALO - 1 + t, 0:gc],
                l2[HALO - 2:HALO - 2 + t, 0:gc],
                l3[HALO - 4:HALO - 4 + t, 0:gc],
                l4[HALO - 8:HALO - 8 + t, :])
    pos = i * t + lax.broadcasted_iota(jnp.int32, (t, 1), 0)
    mixed = []
    for g, w in enumerate(POOL_WINDOWS):
        lo = jnp.clip(pos - w // 2, 0, seq_len)
        hi = jnp.clip(pos - w // 2 + w, 0, seq_len)
        cnt = (hi - lo).astype(_F32)
        pooled = win_sums[g] / cnt - pbuf[HALO:HALO + t, g * gc:(g + 1) * gc]
        mixed.append(jnp.dot(pooled.astype(_BF16), poolw_ref[g], preferred_element_type=_F32))
    pool_o = jnp.concatenate(mixed, axis=-1) * pscale_ref[...]

    m = (g_ref[:, 0:D_MODEL].astype(_F32) * attn_ref[...].astype(_F32)
         + g_ref[:, D_MODEL:2 * D_MODEL].astype(_F32) * conv_o
         + g_ref[:, 2 * D_MODEL:].astype(_F32) * pool_o)
    y = jnp.dot(m.astype(_BF16), wout_ref[...], preferred_element_type=_F32) + bout_ref[...]
    gate = mod_ref[2:3, :]
    o_ref[...] = _layer_norm(DEEPNORM_ALPHA * x_ref[...] + gate * y, lg_ref[...], lb_ref[...])


def _mix(x, attn, a, p, g, mod, per_batch_mod, l, wts, *, tile):
    bsz, n, d = x.shape
    nt = n // tile
    hb = tile // HALO
    last_hb = n // HALO - 1
    mod_idx = (lambda b, i: (l, b, 0, 0)) if per_batch_mod else (lambda b, i: (l, 0, 0, 0))
    tok = lambda w: pl.BlockSpec((None, tile, w), lambda b, i: (b, i, 0))
    prev = pl.BlockSpec((None, HALO, d), lambda b, i: (b, jnp.maximum(i * hb - 1, 0), 0))
    nxt = pl.BlockSpec((None, HALO, d), lambda b, i: (b, jnp.minimum((i + 1) * hb, last_hb), 0))
    vec = _resident((None, 1, d), (l, 0, 0))
    mat = _resident((None, d, d), (l, 0, 0))
    gc = POOL_GROUP_CH
    return pl.pallas_call(
        functools.partial(_mix_kernel, tile=tile, seq_len=n),
        grid=(bsz, nt),
        in_specs=[tok(d), tok(d), tok(d), prev, nxt, tok(d), prev, nxt, tok(3 * d),
                  pl.BlockSpec((None, None, N_MOD, d), mod_idx),
                  _resident((None, CONV_WIDTH, d), (l, 0, 0)), vec, vec, vec, mat, vec,
                  _resident((None, len(POOL_WINDOWS), gc, gc), (l, 0, 0, 0)), vec,
                  mat, vec, vec, vec],
        out_specs=tok(d),
        out_shape=jax.ShapeDtypeStruct((bsz, n, d), _F32),
        scratch_shapes=[
            pltpu.VMEM((tile + 2 * HALO, d), _F32),
            pltpu.VMEM((SUBLANES, tile + 3 * SUBLANES, CONV_LANES), _F32),
            pltpu.VMEM((tile, d), _F32),
            pltpu.VMEM((tile + 2 * HALO + SUBLANES, d), _F32),
            pltpu.VMEM((tile + 4 * SUBLANES, d), _F32),
            pltpu.VMEM((tile + 3 * SUBLANES, d - gc), _F32),
            pltpu.VMEM((tile + 2 * SUBLANES, d - 2 * gc), _F32),
            pltpu.VMEM((tile + SUBLANES, d - 3 * gc), _F32),
        ],
        compiler_params=_params(2),
        name="mix",
    )(x, attn, a, a, a, p, p, p, g, mod,
      wts["conv_dw_w"], wts["conv_dw_b"], wts["conv_ln_g"], wts["conv_ln_b"],
      wts["conv_pw_w"], wts["conv_pw_b"], wts["pool_w"], wts["pool_scale"],
      wts["w_out"], wts["b_out"], wts["ln1_g"], wts["ln1_b"])


def _ffn_kernel(x_ref, xp_ref, xn_ref, mod_ref, wup_ref, dww_ref, dwb_ref, wdn_ref, lg_ref, lb_ref,
                o_ref, hext, hb, abuf, gbuf, *, tile):
    i = pl.program_id(1)
    nt = pl.num_programs(1)
    t = tile
    shift = mod_ref[3:4, :]
    scale = mod_ref[4:5, :]
    gate = mod_ref[5:6, :]

    def modulate(v):
        return v * (1.0 + scale) + shift

    hext[0:FFN_HALO, :] = jnp.where(i > 0, modulate(xp_ref[...]), 0.0)
    hext[FFN_HALO:FFN_HALO + t, :] = modulate(x_ref[...])
    hext[FFN_HALO + t:, :] = jnp.where(i < nt - 1, modulate(xn_ref[...]), 0.0)
    hb[...] = hext[...].astype(_BF16)

    def up(c):
        return jnp.dot(hb[...], wup_ref[c], preferred_element_type=_F32)

    nc = D_FF // FFN_CHUNK
    z = up(0)
    for c in range(nc):
        z_next = up(c + 1) if c + 1 < nc else None
        ab = abuf.at[c % 2]
        ab[...] = z[:, 0:FFN_CHUNK]
        dw = dww_ref[c]
        y = (ab[FFN_HALO - 1:FFN_HALO - 1 + t, :] * dw[0:1, :]
             + ab[FFN_HALO:FFN_HALO + t, :] * dw[1:2, :]
             + ab[FFN_HALO + 1:FFN_HALO + 1 + t, :] * dw[2:3, :]
             + dwb_ref[c])
        u = z[FFN_HALO:FFN_HALO + t, FFN_CHUNK:]
        gbuf[:, c * FFN_CHUNK:(c + 1) * FFN_CHUNK] = (y * _sigmoid(y) * u).astype(_BF16)
        z = z_next
    f = jnp.dot(gbuf[...], wdn_ref[...], preferred_element_type=_F32)
    o_ref[...] = _layer_norm(DEEPNORM_ALPHA * x_ref[...] + gate * f, lg_ref[...], lb_ref[...])


def _ffn(x, mod, per_batch_mod, l, wts, *, tile):
    bsz, n, d = x.shape
    nt = n // tile
    hb = tile // FFN_HALO
    last_hb = n // FFN_HALO - 1
    nc = D_FF // FFN_CHUNK
    mod_idx = (lambda b, i: (l, b, 0, 0)) if per_batch_mod else (lambda b, i: (l, 0, 0, 0))
    tok = pl.BlockSpec((None, tile, d), lambda b, i: (b, i, 0))
    prev = pl.BlockSpec((None, FFN_HALO, d), lambda b, i: (b, jnp.maximum(i * hb - 1, 0), 0))
    nxt = pl.BlockSpec((None, FFN_HALO, d), lambda b, i: (b, jnp.minimum((i + 1) * hb, last_hb), 0))
    vec = _resident((None, 1, d), (l, 0, 0))
    return pl.pallas_call(
        functools.partial(_ffn_kernel, tile=tile),
        grid=(bsz, nt),
        in_specs=[tok, prev, nxt,
                  pl.BlockSpec((None, None, N_MOD, d), mod_idx),
                  _resident((None, nc, d, 2 * FFN_CHUNK), (l, 0, 0, 0)),
                  _resident((None, nc, 3, FFN_CHUNK), (l, 0, 0, 0)),
                  _resident((None, nc, 1, FFN_CHUNK), (l, 0, 0, 0)),
                  _resident((None, D_FF, d), (l, 0, 0)),
                  vec, vec],
        out_specs=tok,
        out_shape=jax.ShapeDtypeStruct((bsz, n, d), _F32),
        scratch_shapes=[
            pltpu.VMEM((tile + 2 * FFN_HALO, d), _F32),
            pltpu.VMEM((tile + 2 * FFN_HALO, d), _BF16),
            pltpu.VMEM((2, tile + 2 * FFN_HALO, FFN_CHUNK), _F32),
            pltpu.VMEM((tile, D_FF), _BF16),
        ],
        compiler_params=_params(2),
        name="conv_ffn",
    )(x, x, x, mod, wts["w_up"], wts["ffn_dw_w"], wts["ffn_dw_b"], wts["w_down"],
      wts["ln2_g"], wts["ln2_b"])


def _rope_tables(n_tokens):
    n_rows = n_tokens // GRID_W
    row = jnp.repeat(jnp.arange(n_rows, dtype=_F32), GRID_W)
    col = jnp.tile(jnp.arange(GRID_W, dtype=_F32), n_rows)
    inv_freq = ROPE_THETA ** (-jnp.arange(ROPE_PAIRS, dtype=_F32) / ROPE_PAIRS)
    ang = jnp.stack([row, col], axis=-1)[..., None] * inv_freq
    cos, sin = jnp.cos(ang), jnp.sin(ang)
    cos_t = jnp.concatenate([cos[:, 0], cos[:, 0], cos[:, 1], cos[:, 1]], axis=-1)
    sin_t = jnp.concatenate([-sin[:, 0], sin[:, 0], -sin[:, 1], sin[:, 1]], axis=-1)
    return cos_t, sin_t


def _pick_tile(n, target):
    return target if n % target == 0 else n


def kernel(x, c, ctx, c_ctx, w_ada, b_ada, w_in, b_in, q_gain, k_gain, conv_dw_w, conv_dw_b,
           conv_ln_g, conv_ln_b, conv_pw_w, conv_pw_b, pool_w, pool_scale, w_out, b_out,
           ln1_g, ln1_b, ln2_g, ln2_b, w_up, ffn_dw_w, ffn_dw_b, w_down):
    bsz, n, d = x.shape
    n_ctx = ctx.shape[1]
    nl = w_in.shape[0]
    nc = D_FF // FFN_CHUNK
    assert d == D_MODEL and n % GRID_W == 0 and n % CONV_ROWS == 0 and n_ctx % CONV_ROWS == 0

    n_cond = -(-(bsz + 1) // SUBLANES) * SUBLANES
    cond = jnp.zeros((n_cond, d), _F32).at[:bsz].set(c).at[bsz].set(c_ctx)
    mods = _ada_all_layers(cond, w_ada, b_ada)
    mod_lat = mods[:, :bsz].reshape(nl, bsz, N_MOD, d)
    mod_ctx = mods[:, bsz:bsz + 1].reshape(nl, 1, N_MOD, d)

    vec = lambda v: v.reshape(nl, 1, v.shape[-1])
    w_up_a = w_up[:, :, :D_FF].reshape(nl, d, nc, FFN_CHUNK)
    w_up_u = w_up[:, :, D_FF:].reshape(nl, d, nc, FFN_CHUNK)
    wts = {
        "conv_dw_w": conv_dw_w, "conv_dw_b": vec(conv_dw_b),
        "conv_ln_g": vec(conv_ln_g), "conv_ln_b": vec(conv_ln_b),
        "conv_pw_w": conv_pw_w.astype(_BF16), "conv_pw_b": vec(conv_pw_b),
        "pool_w": pool_w.astype(_BF16), "pool_scale": vec(pool_scale),
        "w_out": w_out.astype(_BF16), "b_out": vec(b_out),
        "ln1_g": vec(ln1_g), "ln1_b": vec(ln1_b), "ln2_g": vec(ln2_g), "ln2_b": vec(ln2_b),
        "w_up": jnp.concatenate([w_up_a, w_up_u], axis=-1).transpose(0, 2, 1, 3).astype(_BF16),
        "ffn_dw_w": ffn_dw_w.reshape(nl, 3, nc, FFN_CHUNK).transpose(0, 2, 1, 3),
        "ffn_dw_b": ffn_dw_b.reshape(nl, nc, 1, FFN_CHUNK),
        "w_down": w_down.astype(_BF16),
    }
    w_in_b = w_in.astype(_BF16)
    b_in_v, q_gain_v, k_gain_v = vec(b_in), vec(q_gain), vec(k_gain)
    cos_t, sin_t = _rope_tables(n)
    ones_t = jnp.ones((n_ctx, HEAD_DIM), _F32)

    t_lat = _pick_tile(n, 512)
    t_ctx = _pick_tile(n_ctx, 512)
    t_big = _pick_tile(n, 1024)

    for l in range(nl):
        last = l == nl - 1
        inproj = functools.partial(_inproj, l=l, w_in=w_in_b, b_in=b_in_v, q_gain=q_gain_v,
                                   k_gain=k_gain_v)
        q_l, k_l, v_l, a_l, p_l, g_l = inproj(x, mod_lat, True, cos_t=cos_t, sin_t=sin_t,
                                              use_rope=True, kv_only=False, tile=t_lat)
        if last:
            k_c, v_c = inproj(ctx, mod_ctx, False, cos_t=ones_t, sin_t=ones_t,
                              use_rope=False, kv_only=True, tile=t_ctx)
        else:
            q_c, k_c, v_c, a_c, p_c, g_c = inproj(ctx, mod_ctx, False, cos_t=ones_t, sin_t=ones_t,
                                                  use_rope=False, kv_only=False, tile=t_ctx)
        attn_l = _attention(q_l, [(k_c, v_c), (k_l, v_l)], tile=t_big)
        x = _mix(x, attn_l, a_l, p_l, g_l, mod_lat, True, l, wts, tile=t_lat)
        x = _ffn(x, mod_lat, True, l, wts, tile=t_big)
        if not last:
            attn_c = _attention(q_c, [(k_c, v_c)], tile=t_ctx)
            ctx = _mix(ctx, attn_c, a_c, p_c, g_c, mod_ctx, False, l, wts, tile=t_ctx)
            ctx = _ffn(ctx, mod_ctx, False, l, wts, tile=t_ctx)
    return x
```
